```python
import jax, jax.numpy as jnp
from jax import lax
import numpy as np

D_MODEL = 1024
BATCH = 16
SEQ = 4096
DEPTH = 1
DEC_BATCH = 128
DEC_SEQ = 4
PAST_LEN = 8192
PAGE_SIZE = 128

HEAD_DIM = 64
N_HEADS = 8
N_KV_HEADS = 2
GQA = N_HEADS // N_KV_HEADS
ATTN_W = N_HEADS * HEAD_DIM
KV_W = N_KV_HEADS * HEAD_DIM
CMP_LEN = 32
CMP_STRIDE = 16
CMP_HIDDEN = 64
SEL_BLOCK = 64
N_SEL = 16
WINDOW = 512
Q_BLOCK = 32
N_GMLP = 8
GMLP_DIM = 64
GMLP_W = N_GMLP * GMLP_DIM
CHUNK = 128
MIX_W = ATTN_W + GMLP_W
D_FF = 2816
IN_W = ATTN_W + 6 * KV_W + 3 * N_HEADS + 2 * GMLP_W
ALPHA = (2.0 * DEPTH) ** 0.25
BETA = (8.0 * DEPTH) ** -0.25
LN_EPS = 1e-5
NEG = -1e30
BIG = 1e9

kernel_name = "hymba_nsa_gmlp_macaron_step"


def layer_norm(x, g, b):
    xf = x.astype(jnp.float32)
    mu = xf.mean(-1, keepdims=True)
    var = jnp.square(xf - mu).mean(-1, keepdims=True)
    return ((xf - mu) * lax.rsqrt(var + LN_EPS)).astype(x.dtype) * g + b


def rms_norm(x, g):
    xf = x.astype(jnp.float32)
    return (xf * lax.rsqrt(jnp.square(xf).mean(-1, keepdims=True) + LN_EPS)).astype(x.dtype) * g


def swiglu_ffn(x, w_in, w_out):
    a, b = jnp.split(x @ w_in, 2, axis=-1)
    return (jax.nn.silu(a) * b) @ w_out


def half_ffn_block(h, w_in, w_out, g, b):
    return layer_norm(ALPHA * h + 0.5 * swiglu_ffn(h, w_in, w_out), g, b)


def masked_softmax(s, mask):
    p = jax.nn.softmax(jnp.where(mask, s.astype(jnp.float32), NEG), axis=-1)
    return jnp.where(mask, p, 0.0)


def alibi_slopes():
    h = jnp.arange(1, N_HEADS + 1, dtype=jnp.float32)
    return (2.0 ** (-8.0 * h / N_HEADS)).reshape(N_KV_HEADS, GQA)


def split_projection(h, w_in, sgu_g, sgu_b):
    B, T = h.shape[:2]
    sizes = (ATTN_W,) + (KV_W,) * 6 + (3 * N_HEADS, GMLP_W, GMLP_W)
    offs = np.cumsum(sizes)[:-1].tolist()
    q, kc, vc, ks, vs, kw, vw, g, u, v = jnp.split(h @ w_in, offs, axis=-1)
    kv = lambda a: a.reshape(B, T, N_KV_HEADS, HEAD_DIM)
    q = q.reshape(B, T, N_KV_HEADS, GQA, HEAD_DIM)
    gates = jax.nn.sigmoid(g).reshape(B, T, 3, N_KV_HEADS, GQA)
    u = jax.nn.gelu(u)
    v = layer_norm(jax.nn.gelu(v), sgu_g, sgu_b)
    return q, gates, kv(kc), kv(vc), kv(ks), kv(vs), kv(kw), kv(vw), u, v


def compress(kv, pos, w1, b1, w2):
    B, L = kv.shape[:2]
    n_half = L // CMP_STRIDE
    halves = kv[:, :n_half * CMP_STRIDE].reshape(B, n_half, CMP_STRIDE, N_KV_HEADS, HEAD_DIM)
    first = halves[:, :-1] + pos[:CMP_STRIDE, None, :]
    second = halves[:, 1:] + pos[CMP_STRIDE:, None, :]
    hid = (jnp.einsum('bnlkd,ldh->bnkh', first, w1[:CMP_STRIDE])
           + jnp.einsum('bnlkd,ldh->bnkh', second, w1[CMP_STRIDE:]) + b1)
    return jnp.einsum('bnkh,hd->bnkd', jax.nn.gelu(hid), w2)


def to_blocks(kv):
    B, L = kv.shape[:2]
    n_sel = -(-L // SEL_BLOCK)
    kv = jnp.pad(kv, ((0, 0), (0, n_sel * SEL_BLOCK - L), (0, 0), (0, 0)))
    return kv.reshape(B, n_sel, SEL_BLOCK, N_KV_HEADS, HEAD_DIM).transpose(0, 3, 1, 2, 4)


def nsa_block(q, gates, t_pos, kc, vc, ks_blk, vs_blk, kw, vw, w_pos, slopes):
    B, T = q.shape[:2]
    scale = HEAD_DIM ** -0.5
    n_cmp, n_sel = kc.shape[1], ks_blk.shape[2]
    c_end = jnp.arange(n_cmp) * CMP_STRIDE + (CMP_LEN - 1)
    dist_c = t_pos[:, None] - c_end[None, :]
    s = jnp.einsum('btkgd,bnkd->bkgtn', q, kc) * scale
    s = s - slopes[:, :, None, None] * dist_c.astype(jnp.float32)
    p_cmp = masked_softmax(s, dist_c >= 0)
    o_cmp = jnp.einsum('bkgtn,bnkd->btkgd', p_cmp.astype(vc.dtype), vc)
    j = jnp.arange(n_sel)
    overlap = ((j[None, :] * SEL_BLOCK <= c_end[:, None])
               & (j[None, :] * SEL_BLOCK + SEL_BLOCK > c_end[:, None] + 1 - CMP_LEN)).astype(jnp.float32)
    imp = jnp.einsum('bkgtn,ns->bkts', p_cmp, overlap)
    cur = t_pos // SEL_BLOCK
    forced = (j[None, :] == 0) | (j[None, :] == cur[:, None]) | (j[None, :] == cur[:, None] - 1)
    score = jnp.where(forced, BIG, jnp.where(j[None, :] <= cur[:, None], imp, -BIG))
    _, idx = lax.top_k(score, min(N_SEL, n_sel))
    gather = jax.vmap(jax.vmap(lambda blocks, ids: blocks[ids]))
    ks_g = gather(ks_blk, idx)
    vs_g = gather(vs_blk, idx)
    key_pos = idx[..., None] * SEL_BLOCK + jnp.arange(SEL_BLOCK)
    dist_s = t_pos[None, None, :, None, None] - key_pos
    s = jnp.einsum('btkgd,bktnsd->bkgtns', q, ks_g) * scale
    s = s - slopes[:, :, None, None, None] * dist_s[:, :, None].astype(jnp.float32)
    shp = s.shape
    p_slc = masked_softmax(s.reshape(*shp[:4], -1),
                           (dist_s >= 0).reshape(B, N_KV_HEADS, T, -1)[:, :, None])
    o_slc = jnp.einsum('bkgtns,bktnsd->btkgd', p_slc.reshape(shp).astype(vs_g.dtype), vs_g)
    dist_w = t_pos[:, None] - w_pos[None, :]
    valid_w = (dist_w >= 0) & (dist_w <= WINDOW) & (w_pos[None, :] >= 0)
    s = jnp.einsum('btkgd,bskd->bkgts', q, kw) * scale
    s = s - slopes[:, :, None, None] * dist_w.astype(jnp.float32)
    p_win = masked_softmax(s, valid_w)
    o_win = jnp.einsum('bkgts,bskd->btkgd', p_win.astype(vw.dtype), vw)
    out = (gates[:, :, 0, :, :, None] * o_cmp + gates[:, :, 1, :, :, None] * o_slc
           + gates[:, :, 2, :, :, None] * o_win)
    return out.reshape(B, T, ATTN_W)


def nsa_prompt(q, gates, kc, vc, ks, vs, kw, vw, cmp_k, cmp_v, slopes):
    B, L = q.shape[:2]
    kcmp, vcmp = compress(kc, *cmp_k), compress(vc, *cmp_v)
    ks_blk, vs_blk = to_blocks(ks), to_blocks(vs)
    pad = ((0, 0), (WINDOW, 0), (0, 0), (0, 0))
    kw_pad, vw_pad = jnp.pad(kw, pad), jnp.pad(vw, pad)

    def one_block(i):
        start = i * Q_BLOCK
        q_b = lax.dynamic_slice_in_dim(q, start, Q_BLOCK, axis=1)
        g_b = lax.dynamic_slice_in_dim(gates, start, Q_BLOCK, axis=1)
        kw_b = lax.dynamic_slice_in_dim(kw_pad, start, WINDOW + Q_BLOCK, axis=1)
        vw_b = lax.dynamic_slice_in_dim(vw_pad, start, WINDOW + Q_BLOCK, axis=1)
        t_pos = start + jnp.arange(Q_BLOCK)
        w_pos = start - WINDOW + jnp.arange(WINDOW + Q_BLOCK)
        return nsa_block(q_b, g_b, t_pos, kcmp, vcmp, ks_blk, vs_blk, kw_b, vw_b, w_pos, slopes)

    out = lax.map(one_block, jnp.arange(L // Q_BLOCK))
    return out.transpose(1, 0, 2, 3).reshape(B, L, ATTN_W)


def nsa_sample(q, gates, kc, vc, ks, vs, kw, vw, cmp_k, cmp_v, slopes):
    L, T, TW = kc.shape[1], q.shape[1], kw.shape[1]
    t_pos = (L - T) + jnp.arange(T)
    w_pos = (L - TW) + jnp.arange(TW)
    return nsa_block(q, gates, t_pos, compress(kc, *cmp_k), compress(vc, *cmp_v),
                     to_blocks(ks), to_blocks(vs), kw, vw, w_pos, slopes)


def gmlp_mix(u, v, w_s, b_s):
    B, T = u.shape[:2]
    rows = min(T, CHUNK)
    u = u.reshape(B, T // rows, rows, N_GMLP, GMLP_DIM)
    v = v.reshape(B, T // rows, rows, N_GMLP, GMLP_DIM)
    w = jnp.tril(w_s[:, :rows, :rows])
    z = jnp.einsum('hts,bcshd->bcthd', w, v) + b_s[:, :rows].T[:, :, None]
    return (u * z).reshape(B, T, GMLP_W)


def merge_groups(o_attn, o_gmlp, g_out, w_out):
    o = jnp.concatenate([rms_norm(o_attn, g_out[:ATTN_W]), rms_norm(o_gmlp, g_out[ATTN_W:])], axis=-1)
    return o @ w_out


def gather_pages(cache, page_table):
    n_seq, n_pages = page_table.shape
    return cache[page_table].reshape(n_seq, n_pages * PAGE_SIZE, 2, N_KV_HEADS, HEAD_DIM)


def setup_inputs(seed: int = 0) -> dict:
    key = jax.random.key(seed)
    it = iter(jax.random.split(key, 48))

    def nrm(shape, scale=1.0):
        return jax.random.normal(next(it), shape, jnp.float32) * scale

    n_pages = PAST_LEN // PAGE_SIZE
    n_used = DEC_BATCH * n_pages
    n_phys = n_used + -(-n_used // 4)
    win_buf = min(WINDOW, PAST_LEN)
    cache_shape = (DEPTH, n_phys, PAGE_SIZE, 2, N_KV_HEADS, HEAD_DIM)
    D = D_MODEL
    return {
        "x_prompt": nrm((BATCH, SEQ, D)),
        "x_sample": nrm((DEC_BATCH, DEC_SEQ, D)),
        "cache_cmp_kv": nrm(cache_shape),
        "cache_slc_kv": nrm(cache_shape),
        "state_win_kv": nrm((DEPTH, DEC_BATCH, win_buf, 2, N_KV_HEADS, HEAD_DIM)),
        "page_table": jax.random.permutation(next(it), n_phys)[:n_used].reshape(DEC_BATCH, n_pages).astype(jnp.int32),
        "ln1_g": 1.0 + nrm((DEPTH, D), 0.02),
        "ln1_b": nrm((DEPTH, D), 0.02),
        "ffn1_w_in": nrm((DEPTH, D, 2 * D_FF), D ** -0.5),
        "ffn1_w_out": nrm((DEPTH, D_FF, D), BETA * D_FF ** -0.5),
        "w_in": nrm((DEPTH, D, IN_W), D ** -0.5),
        "cmp_k_pos": nrm((DEPTH, CMP_LEN, HEAD_DIM), 0.1),
        "cmp_k_w1": nrm((DEPTH, CMP_LEN, HEAD_DIM, CMP_HIDDEN), (CMP_LEN * HEAD_DIM) ** -0.5),
        "cmp_k_b1": nrm((DEPTH, CMP_HIDDEN), 0.02),
        "cmp_k_w2": nrm((DEPTH, CMP_HIDDEN, HEAD_DIM), CMP_HIDDEN ** -0.5),
        "cmp_v_pos": nrm((DEPTH, CMP_LEN, HEAD_DIM), 0.1),
        "cmp_v_w1": nrm((DEPTH, CMP_LEN, HEAD_DIM, CMP_HIDDEN), (CMP_LEN * HEAD_DIM) ** -0.5),
        "cmp_v_b1": nrm((DEPTH, CMP_HIDDEN), 0.02),
        "cmp_v_w2": nrm((DEPTH, CMP_HIDDEN, HEAD_DIM), CMP_HIDDEN ** -0.5),
        "sgu_ln_g": 1.0 + nrm((DEPTH, GMLP_W), 0.02),
        "sgu_ln_b": nrm((DEPTH, GMLP_W), 0.02),
        "sgu_w": nrm((DEPTH, N_GMLP, CHUNK, CHUNK), CHUNK ** -0.5),
        "sgu_b": 1.0 + nrm((DEPTH, N_GMLP, CHUNK), 0.02),
        "out_norm_g": 1.0 + nrm((DEPTH, MIX_W), 0.02),
        "w_out": nrm((DEPTH, MIX_W, D), BETA * MIX_W ** -0.5),
        "ln2_g": 1.0 + nrm((DEPTH, D), 0.02),
        "ln2_b": nrm((DEPTH, D), 0.02),
        "ffn2_w_in": nrm((DEPTH, D, 2 * D_FF), D ** -0.5),
        "ffn2_w_out": nrm((DEPTH, D_FF, D), BETA * D_FF ** -0.5),
        "ln3_g": 1.0 + nrm((DEPTH, D), 0.02),
        "ln3_b": nrm((DEPTH, D), 0.02),
    }


def reference(x_prompt, x_sample, cache_cmp_kv, cache_slc_kv, state_win_kv, page_table,
              ln1_g, ln1_b, ffn1_w_in, ffn1_w_out, w_in,
              cmp_k_pos, cmp_k_w1, cmp_k_b1, cmp_k_w2, cmp_v_pos, cmp_v_w1, cmp_v_b1, cmp_v_w2,
              sgu_ln_g, sgu_ln_b, sgu_w, sgu_b, out_norm_g, w_out, ln2_g, ln2_b,
              ffn2_w_in, ffn2_w_out, ln3_g, ln3_b):
    slopes = alibi_slopes()
    hp, hs = x_prompt, x_sample
    cmp_p, cmp_s, slc_p, slc_s, win_p, win_s, gv_s = [], [], [], [], [], [], []
    for l in range(DEPTH):
        cmp_k = (cmp_k_pos[l], cmp_k_w1[l], cmp_k_b1[l], cmp_k_w2[l])
        cmp_v = (cmp_v_pos[l], cmp_v_w1[l], cmp_v_b1[l], cmp_v_w2[l])
        hp = half_ffn_block(hp, ffn1_w_in[l], ffn1_w_out[l], ln1_g[l], ln1_b[l])
        hs = half_ffn_block(hs, ffn1_w_in[l], ffn1_w_out[l], ln1_g[l], ln1_b[l])
        q, g, kc, vc, ks, vs, kw, vw, u, v = split_projection(hp, w_in[l], sgu_ln_g[l], sgu_ln_b[l])
        o_attn = nsa_prompt(q, g, kc, vc, ks, vs, kw, vw, cmp_k, cmp_v, slopes)
        o_gmlp = gmlp_mix(u, v, sgu_w[l], sgu_b[l])
        mix = merge_groups(o_attn, o_gmlp, out_norm_g[l], w_out[l])
        hp = layer_norm(ALPHA * hp + mix, ln2_g[l], ln2_b[l])
        cmp_p.append(jnp.stack([kc, vc], axis=2))
        slc_p.append(jnp.stack([ks, vs], axis=2))
        win_p.append(jnp.stack([kw, vw], axis=2)[:, -min(WINDOW, SEQ):])
        q, g, kc, vc, ks, vs, kw, vw, u, v = split_projection(hs, w_in[l], sgu_ln_g[l], sgu_ln_b[l])
        past_cmp = gather_pages(cache_cmp_kv[l], page_table)
        past_slc = gather_pages(cache_slc_kv[l], page_table)
        win_all = jnp.concatenate([state_win_kv[l], jnp.stack([kw, vw], axis=2)], axis=1)
        o_attn = nsa_sample(q, g,
                            jnp.concatenate([past_cmp[:, :, 0], kc], axis=1),
                            jnp.concatenate([past_cmp[:, :, 1], vc], axis=1),
                            jnp.concatenate([past_slc[:, :, 0], ks], axis=1),
                            jnp.concatenate([past_slc[:, :, 1], vs], axis=1),
                            win_all[:, :, 0], win_all[:, :, 1], cmp_k, cmp_v, slopes)
        o_gmlp = gmlp_mix(u, v, sgu_w[l], sgu_b[l])
        mix = merge_groups(o_attn, o_gmlp, out_norm_g[l], w_out[l])
        hs = layer_norm(ALPHA * hs + mix, ln2_g[l], ln2_b[l])
        cmp_s.append(jnp.stack([kc, vc], axis=2))
        slc_s.append(jnp.stack([ks, vs], axis=2))
        win_s.append(win_all[:, -min(WINDOW, PAST_LEN + DEC_SEQ):])
        gv_s.append(v.reshape(v.shape[0], v.shape[1], N_GMLP, GMLP_DIM))
        hp = half_ffn_block(hp, ffn2_w_in[l], ffn2_w_out[l], ln3_g[l], ln3_b[l])
        hs = half_ffn_block(hs, ffn2_w_in[l], ffn2_w_out[l], ln3_g[l], ln3_b[l])
    new_cmp_prompt = jnp.stack(cmp_p, axis=0)
    new_cmp_sample = jnp.stack(cmp_s, axis=0)
    new_slc_prompt = jnp.stack(slc_p, axis=0)
    new_slc_sample = jnp.stack(slc_s, axis=0)
    new_win_prompt = jnp.stack(win_p, axis=0)
    new_win_sample = jnp.stack(win_s, axis=0)
    new_gmlp_v_sample = jnp.stack(gv_s, axis=0)
    return (hp, hs, new_cmp_prompt, new_cmp_sample, new_slc_prompt, new_slc_sample,
            new_win_prompt, new_win_sample, new_gmlp_v_sample)
```

```python
import functools

import numpy as np
import jax
import jax.numpy as jnp
from jax import lax
from jax.experimental import pallas as pl
from jax.experimental.pallas import tpu as pltpu

HEAD_DIM = 64
N_HEADS = 8
N_KV_HEADS = 2
GQA = N_HEADS // N_KV_HEADS
ATTN_W = N_HEADS * HEAD_DIM
KV_W = N_KV_HEADS * HEAD_DIM
CMP_LEN = 32
CMP_STRIDE = 16
CMP_HIDDEN = 64
SEL_BLOCK = 64
N_SEL = 16
WINDOW = 512
N_GMLP = 8
GMLP_DIM = 64
GMLP_W = N_GMLP * GMLP_DIM
CHUNK = 128
PAGE_SIZE = 128
LN_EPS = 1e-5
NEG = -1e30
BIG = 1e9

LANES = 128
HALF = LANES // 2
VMEM_LIMIT = 56 * 1024 * 1024

F32 = jnp.float32
BF16 = jnp.bfloat16

_Q0, _KV0, _G0, _U0, _V0, _PW = 0, ATTN_W, ATTN_W + 6 * KV_W, ATTN_W + 6 * KV_W + LANES, \
    ATTN_W + 6 * KV_W + LANES + GMLP_W, ATTN_W + 6 * KV_W + LANES + 2 * GMLP_W


def _dot(a, b):
    return jnp.dot(a, b, preferred_element_type=F32)


def _dot_nt(a, b):
    return lax.dot_general(a, b, (((1,), (1,)), ((), ())), preferred_element_type=F32)


def _layer_norm(y, g, b):
    mu = jnp.mean(y, axis=-1, keepdims=True)
    d = y - mu
    var = jnp.mean(d * d, axis=-1, keepdims=True)
    return d * lax.rsqrt(var + LN_EPS) * g + b


def _rms_norm(y, g):
    return y * lax.rsqrt(jnp.mean(y * y, axis=-1, keepdims=True) + LN_EPS) * g


def _const_spec(shape):
    nd = len(shape)
    return pl.BlockSpec(shape, lambda *_: (0,) * nd, pipeline_mode=pl.Buffered(1))


def _params(n_grid):
    return pltpu.CompilerParams(dimension_semantics=("arbitrary",) * n_grid, vmem_limit_bytes=VMEM_LIMIT)


def _ffn_body(*refs, alpha, merge):
    if merge:
        (h_ref, oa_ref, og_ref, wm_ref, gm_ref, bm_ref,
         wa_ref, wb_ref, wo_ref, g_ref, b_ref, o_ref) = refs
        cat = jnp.concatenate([oa_ref[...], og_ref[...]], axis=1)
        x = _layer_norm(alpha * h_ref[...] + _dot(cat, wm_ref[...]), gm_ref[...], bm_ref[...])
    else:
        x_ref, wa_ref, wb_ref, wo_ref, g_ref, b_ref, o_ref = refs
        x = x_ref[...]
    xb = x.astype(BF16)
    a = _dot(xb, wa_ref[...])
    b = _dot(xb, wb_ref[...])
    hid = ((a * jax.nn.sigmoid(a)) * b).astype(BF16)
    y = alpha * x + 0.5 * _dot(hid, wo_ref[...])
    o_ref[...] = _layer_norm(y, g_ref[...], b_ref[...])


def _ffn_ln(x, wa, wb, wo, g, b, alpha, merge_args=None, tm=512):
    t, d = x.shape
    tm = min(tm, t)
    assert t % tm == 0
    row = lambda w: pl.BlockSpec((tm, w), lambda i: (i, 0))
    ins, specs = [x], [row(d)]
    if merge_args is not None:
        oa, og, wm, gm, bm = merge_args
        ins += [oa, og, wm, gm, bm]
        specs += [row(oa.shape[1]), row(og.shape[1]), _const_spec(wm.shape), _const_spec(gm.shape),
                  _const_spec(bm.shape)]
    ins += [wa, wb, wo, g, b]
    specs += [_const_spec(a.shape) for a in (wa, wb, wo, g, b)]
    return pl.pallas_call(
        functools.partial(_ffn_body, alpha=alpha, merge=merge_args is not None),
        grid=(t // tm,), in_specs=specs, out_specs=row(d),
        out_shape=jax.ShapeDtypeStruct((t, d), F32), compiler_params=_params(1),
        name="ffn_merge_ln" if merge_args is not None else "ffn_ln",
    )(*ins)


def _proj_body(h_ref, w_ref, sg_ref, sb_ref, mix_ref, mb_ref, go_ref,
               qpad_ref, cmp_ref, slc_ref, win_ref, ksa_ref, vsd_ref, kwp_ref, vwd_ref,
               gate_ref, ogn_ref, v_ref, *, tm, seq):
    p = _dot(h_ref[...].astype(BF16), w_ref[...])
    lane = lax.broadcasted_iota(jnp.int32, (tm, LANES), 1)
    lo = lane < HALF
    for i in range(ATTN_W // LANES):
        x = p[:, _Q0 + i * LANES:_Q0 + (i + 1) * LANES]
        xr = pltpu.roll(x, HALF, axis=1)
        qpad_ref[:, (2 * i) * LANES:(2 * i + 1) * LANES] = jnp.where(lo, x, 0.0).astype(BF16)
        qpad_ref[:, (2 * i + 1) * LANES:(2 * i + 2) * LANES] = jnp.where(lo, xr, 0.0).astype(BF16)
    cmp_ref[...] = p[:, _KV0:_KV0 + 2 * KV_W]
    slc_ref[...] = p[:, _KV0 + 2 * KV_W:_KV0 + 4 * KV_W]
    win_ref[...] = p[:, _KV0 + 4 * KV_W:_KV0 + 6 * KV_W]
    tok = lax.rem(pl.program_id(0) * tm, seq) + lax.broadcasted_iota(jnp.int32, (tm, LANES), 0)
    onehot = _ones_where(lane - HALF == (tok >> _log2(SEL_BLOCK)))
    ks = p[:, _KV0 + 2 * KV_W:_KV0 + 3 * KV_W]
    ksr = pltpu.roll(ks, HALF, axis=1)
    ksa_ref[:, :LANES] = jnp.where(lo, ks, onehot).astype(BF16)
    ksa_ref[:, LANES:] = jnp.where(lo, ksr, onehot).astype(BF16)
    vs = p[:, _KV0 + 3 * KV_W:_KV0 + 4 * KV_W]
    vsr = pltpu.roll(vs, HALF, axis=1)
    vsd_ref[:, :LANES] = jnp.where(lo, vs, vsr).astype(BF16)
    vsd_ref[:, LANES:] = jnp.where(lo, vsr, vs).astype(BF16)
    kw = p[:, _KV0 + 4 * KV_W:_KV0 + 5 * KV_W]
    kwr = pltpu.roll(kw, HALF, axis=1)
    kwp_ref[:, :LANES] = jnp.where(lo, kw, 0.0).astype(BF16)
    kwp_ref[:, LANES:] = jnp.where(lo, kwr, 0.0).astype(BF16)
    vw = p[:, _KV0 + 5 * KV_W:_KV0 + 6 * KV_W]
    vwr = pltpu.roll(vw, HALF, axis=1)
    vwd_ref[:, :LANES] = jnp.where(lo, vw, vwr).astype(BF16)
    vwd_ref[:, LANES:] = jnp.where(lo, vwr, vw).astype(BF16)
    gate_ref[...] = jax.nn.sigmoid(p[:, _G0:_G0 + LANES])
    u = jax.nn.gelu(p[:, _U0:_U0 + GMLP_W])
    v = _layer_norm(jax.nn.gelu(p[:, _V0:_V0 + GMLP_W]), sg_ref[...], sb_ref[...])
    v_ref[...] = v
    vb = v.astype(BF16)
    lo_c = lax.broadcasted_iota(jnp.int32, (CHUNK, LANES), 1) < HALF
    for c in range(tm // CHUNK):
        rows = slice(c * CHUNK, (c + 1) * CHUNK)
        zs = []
        for i in range(GMLP_W // LANES):
            vp = vb[rows, i * LANES:(i + 1) * LANES]
            zs.append(jnp.where(lo_c, _dot(mix_ref[2 * i], vp), _dot(mix_ref[2 * i + 1], vp)))
        z = jnp.concatenate(zs, axis=1) + mb_ref[...]
        ogn_ref[rows, :] = _rms_norm(u[rows, :] * z, go_ref[...]).astype(BF16)


def _proj(h, w, sg, sb, mix, mb, go, seq, tm=256):
    t, d = h.shape
    tm = min(tm, t)
    assert t % tm == 0 and tm % CHUNK == 0 and seq % tm == 0 and seq // SEL_BLOCK <= HALF
    row = lambda width: pl.BlockSpec((tm, width), lambda i: (i, 0))
    widths = [(2 * ATTN_W, BF16), (2 * KV_W, F32), (2 * KV_W, F32), (2 * KV_W, F32), (2 * LANES, BF16),
              (2 * LANES, BF16), (2 * LANES, BF16), (2 * LANES, BF16), (LANES, F32), (GMLP_W, BF16),
              (GMLP_W, F32)]
    return pl.pallas_call(
        functools.partial(_proj_body, tm=tm, seq=seq),
        grid=(t // tm,),
        in_specs=[row(d)] + [_const_spec(a.shape) for a in (w, sg, sb, mix, mb, go)],
        out_specs=[row(wd) for wd, _ in widths],
        out_shape=[jax.ShapeDtypeStruct((t, wd), dt) for wd, dt in widths],
        compiler_params=_params(1), name="proj",
    )(h, w, sg, sb, mix, mb, go)


def _compress_math(x, wab_ref, pos_ref, b1_ref, w2_ref):
    nh = x.shape[0]
    hw = b1_ref.shape[1]
    ab = _dot(x.astype(BF16), wab_ref[...])
    pc = _dot(pos_ref[...], wab_ref[...])
    c = pc[0:1, :hw] + pc[8:9, hw:] + b1_ref[...]
    hid = ab[:, :hw] + pltpu.roll(ab[:, hw:], nh - 1, axis=0) + c
    return _dot(jax.nn.gelu(hid).astype(BF16), w2_ref[...])


def _compress_body(x_ref, wab_ref, pos_ref, b1_ref, w2_ref, o_ref):
    o_ref[0] = _compress_math(x_ref[0], wab_ref, pos_ref, b1_ref, w2_ref).astype(o_ref.dtype)


def _compress_prompt(x4, wab, pos, b1, w2):
    b, nh, kw = x4.shape
    return pl.pallas_call(
        _compress_body, grid=(b,),
        in_specs=[pl.BlockSpec((1, nh, kw), lambda i: (i, 0, 0))] + [_const_spec(a.shape) for a in (wab, pos, b1, w2)],
        out_specs=pl.BlockSpec((1, nh, w2.shape[1]), lambda i: (i, 0, 0)),
        out_shape=jax.ShapeDtypeStruct((b, nh, w2.shape[1]), BF16),
        compiler_params=_params(1), name="compress_prompt",
    )(x4, wab, pos, b1, w2)


def _page_copies(pt_ref, cache_ref, buf_ref, sem_ref, seq_idx, slot, n_pages, rows):
    return [pltpu.make_async_copy(cache_ref.at[pt_ref[seq_idx, j]],
                                  buf_ref.at[slot, pl.ds(j * rows, rows)], sem_ref.at[slot])
            for j in range(n_pages)]


def _gather_step(pt_ref, cache_ref, buf_ref, sem_ref, n_pages, rows):
    i, n = pl.program_id(0), pl.num_programs(0)
    slot = lax.rem(i, 2)

    @pl.when(i == 0)
    def _():
        for cp in _page_copies(pt_ref, cache_ref, buf_ref, sem_ref, i, slot, n_pages, rows):
            cp.start()

    @pl.when(i + 1 < n)
    def _():
        for cp in _page_copies(pt_ref, cache_ref, buf_ref, sem_ref, i + 1, 1 - slot, n_pages, rows):
            cp.start()

    for cp in _page_copies(pt_ref, cache_ref, buf_ref, sem_ref, i, slot, n_pages, rows):
        cp.wait()
    return slot


def _compress_sample_body(pt_ref, cache_ref, wab_ref, pos_ref, b1_ref, w2_ref, o_ref, buf_ref, sem_ref,
                          *, n_pages, rows):
    slot = _gather_step(pt_ref, cache_ref, buf_ref, sem_ref, n_pages, rows)
    o_ref[0] = _compress_math(buf_ref[slot], wab_ref, pos_ref, b1_ref, w2_ref).astype(o_ref.dtype)


def _compress_sample(page_table, cache4, wab, pos, b1, w2):
    nseq, n_pages = page_table.shape
    _, rows, kw = cache4.shape
    nh = n_pages * rows
    cs = lambda a: pl.BlockSpec(a.shape, lambda i, pt: (0,) * a.ndim, pipeline_mode=pl.Buffered(1))
    return pl.pallas_call(
        functools.partial(_compress_sample_body, n_pages=n_pages, rows=rows),
        grid_spec=pltpu.PrefetchScalarGridSpec(
            num_scalar_prefetch=1, grid=(nseq,),
            in_specs=[pl.BlockSpec(memory_space=pl.ANY)] + [cs(a) for a in (wab, pos, b1, w2)],
            out_specs=pl.BlockSpec((1, nh, w2.shape[1]), lambda i, pt: (i, 0, 0)),
            scratch_shapes=[pltpu.VMEM((2, nh, kw), F32), pltpu.SemaphoreType.DMA((2,))]),
        out_shape=jax.ShapeDtypeStruct((nseq, nh, w2.shape[1]), BF16),
        compiler_params=_params(1), name="compress_sample",
    )(page_table, cache4, wab, pos, b1, w2)


def _masked_softmax(s, mask):
    sm = jnp.where(mask, s, NEG)
    m = jnp.max(sm, axis=-1, keepdims=True)
    p = jnp.where(mask, jnp.exp(sm - m), 0.0)
    l = jnp.sum(p, axis=-1, keepdims=True)
    return p * (1.0 / jnp.maximum(l, 1e-30))


def _split_bf16(x):
    hi = x.astype(BF16)
    return hi, (x - hi.astype(F32)).astype(BF16)


def _log2(n):
    assert n > 0 and n & (n - 1) == 0, n
    return n.bit_length() - 1


def _alibi_slopes(head_idx):
    out = jnp.zeros(head_idx.shape, F32)
    for h in range(N_HEADS):
        out = jnp.where(head_idx == h, 2.0 ** -(h + 1), out)
    return out


def _ones_where(mask):
    return jnp.where(mask, 1.0, 0.0)


def _nsa_prompt_body(q_ref, gate_ref, ckv_ref, ksa_ref, vsd_ref, kwp_ref, vwd_ref, ov_ref, go_ref,
                     o_ref, m_ref, l_ref, acc_ref, *, tq, tk, nwin):
    t0 = pl.program_id(1) * tq
    rows = GQA * tq
    lane = lax.broadcasted_iota(jnp.int32, (tq, LANES), 1)
    lo = lane < HALF
    t_rel = lax.broadcasted_iota(jnp.int32, (rows, 1), 0) & (tq - 1)
    g_row = lax.broadcasted_iota(jnp.int32, (rows, 1), 0) >> _log2(tq)
    t_q = t0 + lax.broadcasted_iota(jnp.int32, (tq, 1), 0)
    cur = t_q >> _log2(SEL_BLOCK)
    sub = lax.broadcasted_iota(jnp.int32, (8, tq), 0)
    gates = gate_ref[...]
    nh = ckv_ref.shape[1]
    c_end = lax.broadcasted_iota(jnp.int32, (1, nh), 1) * CMP_STRIDE + (CMP_LEN - 1)
    n_full = t0 // tk
    w_start = pl.multiple_of(jnp.maximum(t0 - WINDOW, 0), tq)
    head_out = []
    for kvh in range(N_KV_HEADS):
        col = slice(kvh * LANES, (kvh + 1) * LANES)
        slope = _alibi_slopes(g_row + kvh * GQA)
        qs = [q_ref[:, (kvh * GQA + g) * LANES:(kvh * GQA + g + 1) * LANES] for g in range(GQA)]
        qall = jnp.concatenate(qs, axis=0)
        dist = (t0 + t_rel) - c_end
        s = _dot_nt(qall, ckv_ref[0, :, col]) - slope * dist.astype(F32)
        pn = _masked_softmax(s, dist >= 0)
        o_cmp = _dot(pn.astype(BF16), ckv_ref[0, :, 2 * LANES + kvh * LANES:2 * LANES + (kvh + 1) * LANES])
        psum = pn[0:tq] + pn[tq:2 * tq] + pn[2 * tq:3 * tq] + pn[3 * tq:4 * tq]
        hi, lw = _split_bf16(psum)
        imp = _dot(hi, ov_ref[...]) + _dot(lw, ov_ref[...])
        j = lane - HALF
        forced = (j == 0) | (j == cur) | (j == cur - 1)
        score = jnp.where(forced, BIG, jnp.where(j <= cur, imp, -BIG))
        sc_t = score.T[HALF:, :]
        cnt = [jnp.zeros((8, tq), F32) for _ in range(HALF // 8)]
        for jp in range(HALF):
            r = sc_t[jp:jp + 1, :]
            for v in range(HALF // 8):
                blk = sc_t[8 * v:8 * v + 8, :]
                if 8 * v > jp:
                    beats = _ones_where(r >= blk)
                elif 8 * v + 7 <= jp:
                    beats = _ones_where(r > blk)
                else:
                    beats = _ones_where(r > blk) + jnp.where(r == blk, _ones_where(sub > jp - 8 * v), 0.0)
                cnt[v] = cnt[v] + beats
        bias_t = jnp.where(jnp.concatenate(cnt, axis=0) < float(N_SEL), 0.0, NEG)
        selbias = jnp.concatenate([jnp.zeros((HALF, tq), F32), bias_t], axis=0).T.astype(BF16)
        qsel = jnp.concatenate([jnp.where(lo, q, selbias) for q in qs], axis=0)
        m_ref[...] = jnp.full((rows, 1), NEG, F32)
        l_ref[...] = jnp.zeros((rows, 1), F32)
        acc_ref[...] = jnp.zeros((rows, LANES), F32)

        def sweep(jt, causal):
            k0 = pl.multiple_of(jt * tk, tk)
            kpos = (k0 - t0) + lax.broadcasted_iota(jnp.int32, (1, tk), 1)
            s = _dot_nt(qsel, ksa_ref[pl.ds(k0, tk), col]) + slope * kpos.astype(F32)
            if causal:
                s = jnp.where(kpos <= t_rel, s, NEG)
            m_prev = m_ref[...]
            m_new = jnp.maximum(m_prev, jnp.max(s, axis=-1, keepdims=True))
            a = jnp.exp(m_prev - m_new)
            p = jnp.exp(s - m_new)
            l_ref[...] = a * l_ref[...] + jnp.sum(p, axis=-1, keepdims=True)
            acc_ref[...] = a * acc_ref[...] + _dot(p.astype(BF16), vsd_ref[pl.ds(k0, tk), col])
            m_ref[...] = m_new

        def full_tile(jt, carry):
            sweep(jt, False)
            return carry

        lax.fori_loop(0, n_full, full_tile, 0)
        sweep(n_full, True)
        o_slc = acc_ref[...] * (1.0 / l_ref[...])
        kpos = (w_start - t0) + lax.broadcasted_iota(jnp.int32, (1, nwin), 1)
        s = _dot_nt(qall, kwp_ref[pl.ds(w_start, nwin), col]) + slope * kpos.astype(F32)
        pw = _masked_softmax(s, (kpos <= t_rel) & (kpos >= t_rel - WINDOW))
        o_win = _dot(pw.astype(BF16), vwd_ref[pl.ds(w_start, nwin), col])
        for g in range(GQA):
            h = kvh * GQA + g
            r = slice(g * tq, (g + 1) * tq)
            head_out.append(gates[:, h:h + 1] * o_cmp[r] + gates[:, N_HEADS + h:N_HEADS + h + 1] * o_slc[r]
                            + gates[:, 2 * N_HEADS + h:2 * N_HEADS + h + 1] * o_win[r])
    o = jnp.concatenate([jnp.where(lo, head_out[2 * i], head_out[2 * i + 1]) for i in range(N_HEADS // 2)], axis=1)
    o_ref[...] = _rms_norm(o, go_ref[...]).astype(BF16)


def _nsa_prompt(qpad, gate, ckv, ksa, vsd, kwp, vwd, ov, go, nb, seq, tq=128, tk=256):
    assert seq % tk == 0 and tk % tq == 0 and tq == LANES and seq >= WINDOW + tq
    nwin = WINDOW + tq
    nq = seq // tq
    qrow = lambda w: pl.BlockSpec((tq, w), lambda b, i: (b * nq + i, 0))
    seqblk = lambda w: pl.BlockSpec((seq, w), lambda b, i: (b, 0))
    cs = lambda a: pl.BlockSpec(a.shape, lambda b, i: (0,) * a.ndim, pipeline_mode=pl.Buffered(1))
    rows = GQA * tq
    return pl.pallas_call(
        functools.partial(_nsa_prompt_body, tq=tq, tk=tk, nwin=nwin),
        grid=(nb, nq),
        in_specs=[qrow(qpad.shape[1]), qrow(LANES), pl.BlockSpec((1,) + ckv.shape[1:], lambda b, i: (b, 0, 0)),
                  seqblk(2 * LANES), seqblk(2 * LANES), seqblk(2 * LANES), seqblk(2 * LANES), cs(ov), cs(go)],
        out_specs=qrow(ATTN_W),
        out_shape=jax.ShapeDtypeStruct((nb * seq, ATTN_W), BF16),
        scratch_shapes=[pltpu.VMEM((rows, 1), F32), pltpu.VMEM((rows, 1), F32), pltpu.VMEM((rows, LANES), F32)],
        compiler_params=_params(2), name="nsa_prompt",
    )(qpad, gate, ckv, ksa, vsd, kwp, vwd, ov, go)


T8 = 8


def _nsa_sample_body(pt_ref, cache_ref, q_ref, gate_ref, ckv_ref, win_ref, new_ref, e_ref, ov_ref, go_ref,
                     o_ref, buf_ref, wbuf_ref, sem_ref, *, n_pages, past, n_new, n_cmp):
    slot = _gather_step(pt_ref, cache_ref, buf_ref, sem_ref, n_pages, PAGE_SIZE)
    rows = N_KV_HEADS * GQA * T8
    ridx = lax.broadcasted_iota(jnp.int32, (rows, 1), 0)
    t_new = ridx & (n_new - 1)
    slope = _alibi_slopes(ridx >> _log2(T8))
    q = q_ref[0]
    new = new_ref[0]
    pad_rows = LANES - T8

    def tail(x):
        return jnp.concatenate([x, jnp.zeros((pad_rows, x.shape[1]), F32)], axis=0)

    nh = ckv_ref.shape[1]
    n_i = lax.broadcasted_iota(jnp.int32, (1, nh), 1)
    dist = (past + t_new) - (n_i * CMP_STRIDE + (CMP_LEN - 1))
    ckv = ckv_ref[0]
    s = _dot_nt(q, ckv[:, :LANES]) - slope * dist.astype(F32)
    pn = _masked_softmax(s, (dist >= 0) & (n_i < n_cmp))
    o_cmp = _dot(pn.astype(BF16), ckv[:, LANES:])
    psum = jnp.concatenate(
        [sum(pn[(k * GQA + g) * T8:(k * GQA + g + 1) * T8] for g in range(GQA)) for k in range(N_KV_HEADS)], axis=0)
    hi, lw = _split_bf16(psum)
    imp = _dot(hi, ov_ref[...]) + _dot(lw, ov_ref[...])
    n_blk = past // SEL_BLOCK
    lane = lax.broadcasted_iota(jnp.int32, (N_KV_HEADS * T8, LANES), 1)
    forced = (lane == 0) | (lane == n_blk - 1)
    score = jnp.where(forced, BIG, jnp.where(lane < n_blk, imp, -BIG))
    cnt = jnp.zeros(score.shape, F32)
    for jp in range(n_blk):
        r = score[:, jp:jp + 1]
        cnt = cnt + _ones_where(r > score) + jnp.where(r == score, _ones_where(lane > jp), 0.0)
    bias = jnp.where(cnt < float(N_SEL - 1), 0.0, NEG).astype(BF16)
    selbias = jnp.concatenate([bias[k * T8:(k + 1) * T8] for k in range(N_KV_HEADS) for _ in range(GQA)], axis=0)
    buf_ref[slot, pl.ds(past, LANES), :] = tail(new[:, 2 * KV_W:4 * KV_W])
    nk = past + LANES
    kpos = lax.broadcasted_iota(jnp.int32, (1, nk), 1) - past
    kv = buf_ref[slot]
    s = (_dot_nt(q, kv[:, :LANES].astype(BF16)) + _dot_nt(selbias, e_ref[...])
         + slope * kpos.astype(F32))
    ps = _masked_softmax(s, kpos <= t_new)
    o_slc = _dot(ps.astype(BF16), kv[:, LANES:].astype(BF16))
    nw = win_ref.shape[1]
    wbuf_ref[pl.ds(0, nw), :] = win_ref[0]
    wbuf_ref[pl.ds(nw, LANES), :] = tail(new[:, 4 * KV_W:6 * KV_W])
    wpos = lax.broadcasted_iota(jnp.int32, (1, nw + LANES), 1) - nw
    wkv = wbuf_ref[...]
    s = _dot_nt(q, wkv[:, :LANES].astype(BF16)) + slope * wpos.astype(F32)
    pw = _masked_softmax(s, (wpos <= t_new) & (wpos >= t_new - WINDOW) & (wpos + past >= 0))
    o_win = _dot(pw.astype(BF16), wkv[:, LANES:].astype(BF16))
    gates = gate_ref[0]
    o = gates[:, 0:1] * o_cmp + gates[:, 1:2] * o_slc + gates[:, 2:3] * o_win
    o_r = pltpu.roll(o, HALF, axis=1)
    lo = lax.broadcasted_iota(jnp.int32, (T8, LANES), 1) < HALF
    groups = []
    for k in range(N_KV_HEADS):
        for i in range(GQA // 2):
            r_even = slice((k * GQA + 2 * i) * T8, (k * GQA + 2 * i + 1) * T8)
            r_odd = slice((k * GQA + 2 * i + 1) * T8, (k * GQA + 2 * i + 2) * T8)
            groups.append(jnp.where(lo, o[r_even], o_r[r_odd]) if k == 0 else jnp.where(lo, o_r[r_even], o[r_odd]))
    o_ref[0] = _rms_norm(jnp.concatenate(groups, axis=1), go_ref[...]).astype(BF16)


def _nsa_sample(page_table, cache, q_bd, gate_rows, ckv, win_state, new8, e_mat, ov, go, past, n_new):
    nseq, n_pages = page_table.shape
    assert past == n_pages * PAGE_SIZE and past % SEL_BLOCK == 0 and past // SEL_BLOCK <= LANES
    assert n_new <= T8 and T8 % n_new == 0 and past % CMP_STRIDE == 0
    rows = N_KV_HEADS * GQA * T8
    nw = win_state.shape[1]
    per_seq = lambda a: pl.BlockSpec((1,) + a.shape[1:], lambda i, pt: (i,) + (0,) * (a.ndim - 1))
    cs = lambda a: pl.BlockSpec(a.shape, lambda i, pt: (0,) * a.ndim, pipeline_mode=pl.Buffered(1))
    return pl.pallas_call(
        functools.partial(_nsa_sample_body, n_pages=n_pages, past=past, n_new=n_new,
                          n_cmp=past // CMP_STRIDE - 1),
        grid_spec=pltpu.PrefetchScalarGridSpec(
            num_scalar_prefetch=1, grid=(nseq,),
            in_specs=[pl.BlockSpec(memory_space=pl.ANY), per_seq(q_bd), per_seq(gate_rows), per_seq(ckv),
                      per_seq(win_state), per_seq(new8), cs(e_mat), cs(ov), cs(go)],
            out_specs=pl.BlockSpec((1, T8, ATTN_W), lambda i, pt: (i, 0, 0)),
            scratch_shapes=[pltpu.VMEM((2, past + LANES, 2 * KV_W), F32), pltpu.VMEM((nw + LANES, 2 * KV_W), F32),
                            pltpu.SemaphoreType.DMA((2,))]),
        out_shape=jax.ShapeDtypeStruct((nseq, T8, ATTN_W), BF16),
        compiler_params=_params(1), name="nsa_sample",
    )(page_table, cache, q_bd, gate_rows, ckv, win_state, new8, e_mat, ov, go)


def _overlap(n_rows, n_cmp, lane0):
    n = np.arange(n_rows)[:, None]
    j = np.arange(LANES)[None, :] - lane0
    c_end = n * CMP_STRIDE + (CMP_LEN - 1)
    ov = (j * SEL_BLOCK <= c_end) & (j * SEL_BLOCK + SEL_BLOCK > c_end + 1 - CMP_LEN) & (n < n_cmp) & (j >= 0)
    return jnp.asarray(ov, BF16)


def _compress_weights(k_pos, k_w1, k_b1, k_w2, v_pos, v_w1, v_b1, v_w2):
    eye = jnp.eye(4, dtype=F32)
    w1 = jnp.stack([k_w1, k_w1, v_w1, v_w1])
    half = lambda a: jnp.einsum("cldh,ce->lcdeh", a, eye).reshape(CMP_STRIDE * 4 * HEAD_DIM, 4 * CMP_HIDDEN)
    wab = jnp.concatenate([half(w1[:, :CMP_STRIDE]), half(w1[:, CMP_STRIDE:])], axis=1).astype(BF16)
    pos = jnp.stack([k_pos, k_pos, v_pos, v_pos])
    flat = lambda a: jnp.transpose(a, (1, 0, 2)).reshape(1, -1)
    pos16 = jnp.concatenate([jnp.broadcast_to(flat(pos[:, :CMP_STRIDE]), (8, wab.shape[0])),
                             jnp.broadcast_to(flat(pos[:, CMP_STRIDE:]), (8, wab.shape[0]))], axis=0).astype(BF16)
    b1 = jnp.concatenate([k_b1, k_b1, v_b1, v_b1]).reshape(1, -1)
    z = jnp.zeros((CMP_HIDDEN, HEAD_DIM), F32)
    w2p = jnp.concatenate([
        jnp.concatenate([k_w2, z, z, z, z, z, z, z], axis=1), jnp.concatenate([z, z, k_w2, z, z, z, z, z], axis=1),
        jnp.concatenate([z, z, z, z, v_w2, v_w2, z, z], axis=1), jnp.concatenate([z, z, z, z, z, z, v_w2, v_w2], axis=1),
    ], axis=0).astype(BF16)
    w2s = jnp.concatenate([
        jnp.concatenate([k_w2, z, z, z], axis=1), jnp.concatenate([z, k_w2, z, z], axis=1),
        jnp.concatenate([z, z, v_w2, z], axis=1), jnp.concatenate([z, z, z, v_w2], axis=1)], axis=0).astype(BF16)
    return wab, pos16, b1, w2p, w2s


def _proj_weight(w_in):
    d = w_in.shape[0]
    g0 = ATTN_W + 6 * KV_W
    wg = jnp.concatenate([w_in[:, g0:g0 + 3 * N_HEADS], jnp.zeros((d, LANES - 3 * N_HEADS), F32)], axis=1)
    return jnp.concatenate([w_in[:, :ATTN_W] * HEAD_DIM ** -0.5, w_in[:, ATTN_W:g0], wg,
                            w_in[:, g0 + 3 * N_HEADS:]], axis=1).astype(BF16)


def kernel(x_prompt, x_sample, cache_cmp_kv, cache_slc_kv, state_win_kv, page_table, ln1_g, ln1_b, ffn1_w_in, ffn1_w_out, w_in, cmp_k_pos, cmp_k_w1, cmp_k_b1, cmp_k_w2, cmp_v_pos, cmp_v_w1, cmp_v_b1, cmp_v_w2, sgu_ln_g, sgu_ln_b, sgu_w, sgu_b, out_norm_g, w_out, ln2_g, ln2_b, ffn2_w_in, ffn2_w_out, ln3_g, ln3_b):
    depth = w_in.shape[0]
    nb, seq, d = x_prompt.shape
    ns, n_new, _ = x_sample.shape
    n_pages = page_table.shape[1]
    past = n_pages * PAGE_SIZE
    d_ff = ffn1_w_out.shape[1]
    alpha = (2.0 * depth) ** 0.25
    row = lambda a: a.reshape(1, -1)
    hp = x_prompt.reshape(nb * seq, d)
    hs = x_sample.reshape(ns * n_new, d)
    kv5 = (N_KV_HEADS, HEAD_DIM)
    outs = [[] for _ in range(7)]
    ov_p = _overlap(seq // CMP_STRIDE, seq // CMP_STRIDE - 1, HALF)
    ov_s = _overlap(past // CMP_STRIDE, past // CMP_STRIDE - 1, 0)
    e_mat = jnp.asarray((np.arange(past + LANES)[:, None] // SEL_BLOCK == np.arange(LANES)[None, :])
                        & (np.arange(past + LANES)[:, None] < past), BF16)
    for l in range(depth):
        f1 = (ffn1_w_in[l, :, :d_ff].astype(BF16), ffn1_w_in[l, :, d_ff:].astype(BF16), ffn1_w_out[l].astype(BF16),
              row(ln1_g[l]), row(ln1_b[l]))
        f2 = (ffn2_w_in[l, :, :d_ff].astype(BF16), ffn2_w_in[l, :, d_ff:].astype(BF16), ffn2_w_out[l].astype(BF16),
              row(ln3_g[l]), row(ln3_b[l]))
        wp = _proj_weight(w_in[l])
        wab, pos16, b1, w2p, w2s = _compress_weights(cmp_k_pos[l], cmp_k_w1[l], cmp_k_b1[l], cmp_k_w2[l],
                                                     cmp_v_pos[l], cmp_v_w1[l], cmp_v_b1[l], cmp_v_w2[l])
        sg, sb = row(sgu_ln_g[l]), row(sgu_ln_b[l])
        go_a, go_g = row(out_norm_g[l, :ATTN_W]), row(out_norm_g[l, ATTN_W:])
        wm = w_out[l].astype(BF16)
        mix_p = jnp.tril(sgu_w[l]).astype(BF16)
        mb_p = jnp.repeat(sgu_b[l].T, GMLP_DIM, axis=1)
        r = min(n_new, CHUNK)
        mix_s = jax.vmap(lambda w: jnp.kron(jnp.eye(CHUNK // r, dtype=F32), w))(jnp.tril(sgu_w[l, :, :r, :r])).astype(BF16)
        mb_s = jnp.tile(jnp.repeat(sgu_b[l, :, :r].T, GMLP_DIM, axis=1), (CHUNK // r, 1))

        hp = _ffn_ln(hp, *f1, alpha)
        qpad, cmp_p, slc_p, win_p, ksa, vsd, kwp, vwd, gate, ogn, _ = _proj(hp, wp, sg, sb, mix_p, mb_p, go_g, seq)
        ckv = _compress_prompt(cmp_p.reshape(nb, seq // CMP_STRIDE, CMP_STRIDE * 2 * KV_W), wab, pos16, b1, w2p)
        oan = _nsa_prompt(qpad, gate, ckv, ksa, vsd, kwp, vwd, ov_p, go_a, nb, seq)
        hp = _ffn_ln(hp, *f2, alpha, merge_args=(oan, ogn, wm, row(ln2_g[l]), row(ln2_b[l])))
        outs[0].append(cmp_p.reshape(nb, seq, 2, *kv5))
        outs[2].append(slc_p.reshape(nb, seq, 2, *kv5))
        outs[4].append(win_p.reshape(nb, seq, 2, *kv5)[:, -min(WINDOW, seq):])

        hs = _ffn_ln(hs, *f1, alpha)
        pad_seq = -(-n_new * ns // CHUNK) * CHUNK
        qpad, cmp_s, slc_s, win_s, _, _, _, _, gate, ogn, v_s = _proj(hs, wp, sg, sb, mix_s, mb_s, go_g, pad_seq)
        ckv = _compress_sample(page_table, cache_cmp_kv[l].reshape(-1, PAGE_SIZE // CMP_STRIDE, CMP_STRIDE * 2 * KV_W),
                               wab, pos16, b1, w2s)
        q5 = qpad.reshape(ns, n_new, N_KV_HEADS, GQA, LANES)[..., :HEAD_DIM]
        q5 = jnp.pad(jnp.transpose(q5, (0, 2, 3, 1, 4)), ((0, 0), (0, 0), (0, 0), (0, T8 - n_new), (0, 0)))
        zq = jnp.zeros_like(q5[:, 0])
        q_bd = jnp.stack([jnp.concatenate([q5[:, 0], zq], axis=-1), jnp.concatenate([zq, q5[:, 1]], axis=-1)],
                         axis=1).reshape(ns, N_KV_HEADS * GQA * T8, LANES)
        g5 = gate[:, :3 * N_HEADS].reshape(ns, n_new, 3, N_KV_HEADS, GQA)
        g5 = jnp.pad(jnp.transpose(g5, (0, 3, 4, 1, 2)), ((0, 0), (0, 0), (0, 0), (0, T8 - n_new), (0, 0)))
        gate_rows = g5.reshape(ns, N_KV_HEADS * GQA * T8, 3)
        new8 = jnp.pad(jnp.concatenate([cmp_s, slc_s, win_s], axis=1).reshape(ns, n_new, 6 * KV_W),
                       ((0, 0), (0, T8 - n_new), (0, 0)))
        win_state = state_win_kv[l].reshape(ns, -1, 2 * KV_W)
        oan = _nsa_sample(page_table, cache_slc_kv[l].reshape(-1, PAGE_SIZE, 2 * KV_W), q_bd, gate_rows, ckv,
                          win_state, new8, e_mat, ov_s, go_a, past, n_new)
        oan = oan[:, :n_new].reshape(ns * n_new, ATTN_W)
        hs = _ffn_ln(hs, *f2, alpha, merge_args=(oan, ogn, wm, row(ln2_g[l]), row(ln2_b[l])))
        win_all = jnp.concatenate([win_state, win_s.reshape(ns, n_new, 2 * KV_W)], axis=1)
        outs[1].append(cmp_s.reshape(ns, n_new, 2, *kv5))
        outs[3].append(slc_s.reshape(ns, n_new, 2, *kv5))
        outs[5].append(win_all[:, -min(WINDOW, past + n_new):].reshape(ns, -1, 2, *kv5))
        outs[6].append(v_s.reshape(ns, n_new, N_GMLP, GMLP_DIM))
    st = [jnp.stack(o, axis=0) for o in outs]
    return (hp.reshape(nb, seq, d), hs.reshape(ns, n_new, d), st[0], st[1], st[2], st[3], st[4], st[5], st[6])
```

```python
import functools

import numpy as np
import jax
import jax.numpy as jnp
from jax import lax
from jax.experimental import pallas as pl
from jax.experimental.pallas import tpu as pltpu

HEAD_DIM = 64
N_HEADS = 8
N_KV_HEADS = 2
GQA = N_HEADS // N_KV_HEADS
ATTN_W = N_HEADS * HEAD_DIM
KV_W = N_KV_HEADS * HEAD_DIM
CMP_LEN = 32
CMP_STRIDE = 16
CMP_HIDDEN = 64
SEL_BLOCK = 64
N_SEL = 16
WINDOW = 512
N_GMLP = 8
GMLP_DIM = 64
GMLP_W = N_GMLP * GMLP_DIM
CHUNK = 128
PAGE_SIZE = 128
LN_EPS = 1e-5
NEG = -1e30
BIG = 1e9

LANES = 128
HALF = LANES // 2
VMEM_LIMIT = 56 * 1024 * 1024
SLOPES = [2.0 ** -(h + 1) for h in range(N_HEADS)]

F32 = jnp.float32
BF16 = jnp.bfloat16

_Q0, _KV0, _G0, _U0, _V0, _PW = 0, ATTN_W, ATTN_W + 6 * KV_W, ATTN_W + 6 * KV_W + LANES, \
    ATTN_W + 6 * KV_W + LANES + GMLP_W, ATTN_W + 6 * KV_W + LANES + 2 * GMLP_W


def _dot(a, b):
    return jnp.dot(a, b, preferred_element_type=F32)


def _dot_nt(a, b):
    return lax.dot_general(a, b, (((1,), (1,)), ((), ())), preferred_element_type=F32)


def _layer_norm(y, g, b):
    mu = jnp.mean(y, axis=-1, keepdims=True)
    d = y - mu
    var = jnp.mean(d * d, axis=-1, keepdims=True)
    return d * lax.rsqrt(var + LN_EPS) * g + b


def _rms_norm(y, g):
    return y * lax.rsqrt(jnp.mean(y * y, axis=-1, keepdims=True) + LN_EPS) * g


def _ones_where(mask):
    return jnp.where(mask, 1.0, 0.0)


def _log2(n):
    assert n > 0 and n & (n - 1) == 0, n
    return n.bit_length() - 1


def _const_spec(shape):
    nd = len(shape)
    return pl.BlockSpec(shape, lambda *_: (0,) * nd, pipeline_mode=pl.Buffered(1))


def _params(n_grid):
    return pltpu.CompilerParams(dimension_semantics=("arbitrary",) * n_grid, vmem_limit_bytes=VMEM_LIMIT)


def _ffn_body(*refs, alpha, merge):
    if merge:
        (h_ref, oa_ref, og_ref, wm_ref, gm_ref, bm_ref,
         wa_ref, wb_ref, wo_ref, g_ref, b_ref, o_ref) = refs
        cat = jnp.concatenate([oa_ref[...], og_ref[...]], axis=1)
        x = _layer_norm(alpha * h_ref[...] + _dot(cat, wm_ref[...]), gm_ref[...], bm_ref[...])
    else:
        x_ref, wa_ref, wb_ref, wo_ref, g_ref, b_ref, o_ref = refs
        x = x_ref[...]
    xb = x.astype(BF16)
    a = _dot(xb, wa_ref[...])
    b = _dot(xb, wb_ref[...])
    hid = ((a * jax.nn.sigmoid(a)) * b).astype(BF16)
    y = alpha * x + 0.5 * _dot(hid, wo_ref[...])
    o_ref[...] = _layer_norm(y, g_ref[...], b_ref[...])


def _ffn_ln(x, wa, wb, wo, g, b, alpha, merge_args=None, tm=512):
    t, d = x.shape
    tm = min(tm, t)
    assert t % tm == 0
    row = lambda w: pl.BlockSpec((tm, w), lambda i: (i, 0))
    ins, specs = [x], [row(d)]
    if merge_args is not None:
        oa, og, wm, gm, bm = merge_args
        ins += [oa, og, wm, gm, bm]
        specs += [row(oa.shape[1]), row(og.shape[1]), _const_spec(wm.shape), _const_spec(gm.shape),
                  _const_spec(bm.shape)]
    ins += [wa, wb, wo, g, b]
    specs += [_const_spec(a.shape) for a in (wa, wb, wo, g, b)]
    return pl.pallas_call(
        functools.partial(_ffn_body, alpha=alpha, merge=merge_args is not None),
        grid=(t // tm,), in_specs=specs, out_specs=row(d),
        out_shape=jax.ShapeDtypeStruct((t, d), F32), compiler_params=_params(1),
        name="ffn_merge_ln" if merge_args is not None else "ffn_ln",
    )(*ins)


def _proj_body(h_ref, w_ref, sg_ref, sb_ref, mix_ref, mb_ref, go_ref,
               qpad_ref, cmpk_ref, cmpv_ref, slc_ref, win_ref, cmpt_ref, slct_ref, wint_ref,
               ksa_ref, vsd_ref, kwp_ref, vwd_ref, gate_ref, ogn_ref, v_ref, *, tm, seq):
    p = _dot(h_ref[...].astype(BF16), w_ref[...])
    lane = lax.broadcasted_iota(jnp.int32, (tm, LANES), 1)
    lo = lane < HALF
    for i in range(ATTN_W // LANES):
        x = p[:, _Q0 + i * LANES:_Q0 + (i + 1) * LANES]
        xr = pltpu.roll(x, HALF, axis=1)
        qpad_ref[:, (2 * i) * LANES:(2 * i + 1) * LANES] = jnp.where(lo, x, 0.0).astype(BF16)
        qpad_ref[:, (2 * i + 1) * LANES:(2 * i + 2) * LANES] = jnp.where(lo, xr, 0.0).astype(BF16)
    for s, feat_ref in enumerate((cmpt_ref, slct_ref, wint_ref)):
        feat_ref[0] = p[:, _KV0 + 2 * s * KV_W:_KV0 + 2 * (s + 1) * KV_W].T
    cmpk_ref[...] = p[:, _KV0:_KV0 + KV_W]
    cmpv_ref[...] = p[:, _KV0 + KV_W:_KV0 + 2 * KV_W]
    slc_ref[...] = p[:, _KV0 + 2 * KV_W:_KV0 + 4 * KV_W]
    win_ref[...] = p[:, _KV0 + 4 * KV_W:_KV0 + 6 * KV_W]
    tok = lax.rem(pl.program_id(0) * tm, seq) + lax.broadcasted_iota(jnp.int32, (tm, LANES), 0)
    onehot = _ones_where(lane - HALF == (tok >> _log2(SEL_BLOCK)))
    ks = p[:, _KV0 + 2 * KV_W:_KV0 + 3 * KV_W]
    ksr = pltpu.roll(ks, HALF, axis=1)
    ksa_ref[:, :LANES] = jnp.where(lo, ks, onehot).astype(BF16)
    ksa_ref[:, LANES:] = jnp.where(lo, ksr, onehot).astype(BF16)
    vs = p[:, _KV0 + 3 * KV_W:_KV0 + 4 * KV_W]
    vsr = pltpu.roll(vs, HALF, axis=1)
    vsd_ref[:, :LANES] = jnp.where(lo, vs, vsr).astype(BF16)
    vsd_ref[:, LANES:] = jnp.where(lo, vsr, vs).astype(BF16)
    kw = p[:, _KV0 + 4 * KV_W:_KV0 + 5 * KV_W]
    kwr = pltpu.roll(kw, HALF, axis=1)
    kwp_ref[:, :LANES] = jnp.where(lo, kw, 0.0).astype(BF16)
    kwp_ref[:, LANES:] = jnp.where(lo, kwr, 0.0).astype(BF16)
    vw = p[:, _KV0 + 5 * KV_W:_KV0 + 6 * KV_W]
    vwr = pltpu.roll(vw, HALF, axis=1)
    vwd_ref[:, :LANES] = jnp.where(lo, vw, vwr).astype(BF16)
    vwd_ref[:, LANES:] = jnp.where(lo, vwr, vw).astype(BF16)
    gate_ref[...] = jax.nn.sigmoid(p[:, _G0:_G0 + LANES])
    u = jax.nn.gelu(p[:, _U0:_U0 + GMLP_W])
    v = _layer_norm(jax.nn.gelu(p[:, _V0:_V0 + GMLP_W]), sg_ref[...], sb_ref[...])
    v_ref[...] = v
    vb = v.astype(BF16)
    lo_c = lax.broadcasted_iota(jnp.int32, (CHUNK, LANES), 1) < HALF
    for c in range(tm // CHUNK):
        rows = slice(c * CHUNK, (c + 1) * CHUNK)
        zs = []
        for i in range(GMLP_W // LANES):
            vp = vb[rows, i * LANES:(i + 1) * LANES]
            zs.append(jnp.where(lo_c, _dot(mix_ref[2 * i], vp), _dot(mix_ref[2 * i + 1], vp)))
        z = jnp.concatenate(zs, axis=1) + mb_ref[...]
        ogn_ref[rows, :] = _rms_norm(u[rows, :] * z, go_ref[...]).astype(BF16)


def _proj(h, w, sg, sb, mix, mb, go, seq, tm=256):
    t, d = h.shape
    tm = min(tm, t)
    assert t % tm == 0 and tm % CHUNK == 0 and seq % tm == 0 and t % seq == 0 and seq // SEL_BLOCK <= HALF
    nq = seq // tm
    row = lambda width: pl.BlockSpec((tm, width), lambda i: (i, 0))
    feat = pl.BlockSpec((1, 2 * KV_W, tm), lambda i: (i // nq, 0, i % nq))
    tok_major = [(2 * ATTN_W, BF16), (KV_W, F32), (KV_W, F32), (2 * KV_W, F32), (2 * KV_W, F32)]
    rest = [(2 * LANES, BF16), (2 * LANES, BF16), (2 * LANES, BF16), (2 * LANES, BF16), (LANES, F32),
            (GMLP_W, BF16), (GMLP_W, F32)]
    sds = jax.ShapeDtypeStruct
    return pl.pallas_call(
        functools.partial(_proj_body, tm=tm, seq=seq),
        grid=(t // tm,),
        in_specs=[row(d)] + [_const_spec(a.shape) for a in (w, sg, sb, mix, mb, go)],
        out_specs=[row(wd) for wd, _ in tok_major] + [feat] * 3 + [row(wd) for wd, _ in rest],
        out_shape=[sds((t, wd), dt) for wd, dt in tok_major] + [sds((t // seq, 2 * KV_W, seq), F32)] * 3
        + [sds((t, wd), dt) for wd, dt in rest],
        compiler_params=_params(1), name="proj",
    )(h, w, sg, sb, mix, mb, go)


def _compress_math(rows_at, nh, wab_ref, pos_ref, b1_ref, w2_ref):
    hw = b1_ref.shape[1]
    ab = None
    for l in range(CMP_STRIDE):
        lhs = jnp.concatenate([rows_at(l).astype(BF16), pos_ref[l]], axis=0)
        d = _dot(lhs, wab_ref[l])
        ab = d if ab is None else ab + d
    c = ab[nh:nh + 1, :hw] + ab[nh + 8:nh + 9, hw:] + b1_ref[...]
    hid = ab[:nh, :hw] + pltpu.roll(ab[:nh, hw:], nh - 1, axis=0) + c
    return _dot(jax.nn.gelu(hid).astype(BF16), w2_ref[...])


def _strided_rows(k_ref, v_ref, nh):
    return lambda l: jnp.concatenate([k_ref[pl.ds(l, nh, stride=CMP_STRIDE), :],
                                      v_ref[pl.ds(l, nh, stride=CMP_STRIDE), :]], axis=1)


def _compress_body(xk_ref, xv_ref, wab_ref, pos_ref, b1_ref, w2_ref, o_ref):
    nh = o_ref.shape[1]
    o_ref[0] = _compress_math(_strided_rows(xk_ref, xv_ref, nh), nh, wab_ref, pos_ref, b1_ref, w2_ref).astype(o_ref.dtype)


def _compress_prompt(xk, xv, nb, wab, pos, b1, w2):
    seq = xk.shape[0] // nb
    nh = seq // CMP_STRIDE
    return pl.pallas_call(
        _compress_body, grid=(nb,),
        in_specs=[pl.BlockSpec((seq, KV_W), lambda i: (i, 0))] * 2 + [_const_spec(a.shape) for a in (wab, pos, b1, w2)],
        out_specs=pl.BlockSpec((1, nh, w2.shape[1]), lambda i: (i, 0, 0)),
        out_shape=jax.ShapeDtypeStruct((nb, nh, w2.shape[1]), BF16),
        compiler_params=_params(1), name="compress_prompt",
    )(xk, xv, wab, pos, b1, w2)


def _gather_step(make_copies):
    i, n = pl.program_id(0), pl.num_programs(0)
    slot = lax.rem(i, 2)

    @pl.when(i == 0)
    def _():
        for cp in make_copies(i, slot):
            cp.start()

    @pl.when(i + 1 < n)
    def _():
        for cp in make_copies(i + 1, 1 - slot):
            cp.start()

    for cp in make_copies(i, slot):
        cp.wait()
    return slot


def _compress_sample_body(pt_ref, cache_ref, wab_ref, pos_ref, b1_ref, w2_ref, o_ref, buf_ref, xs_ref, sem_ref,
                          *, n_pages):
    def make_copies(seq_idx, slot):
        return [pltpu.make_async_copy(cache_ref.at[pt_ref[seq_idx, j]], buf_ref.at[slot, j], sem_ref.at[slot])
                for j in range(n_pages)]

    slot = _gather_step(make_copies)

    def to_token_major(j, carry):
        r = pl.ds(pl.multiple_of(j * PAGE_SIZE, PAGE_SIZE), PAGE_SIZE)
        xs_ref[0, r, :] = buf_ref[slot, j, 0].T
        xs_ref[1, r, :] = buf_ref[slot, j, 1].T
        return carry

    lax.fori_loop(0, n_pages, to_token_major, 0)
    nh = o_ref.shape[1]
    rows_at = _strided_rows(xs_ref.at[0], xs_ref.at[1], nh)
    o_ref[0] = _compress_math(rows_at, nh, wab_ref, pos_ref, b1_ref, w2_ref).astype(o_ref.dtype)


def _compress_sample(page_table, cache_t, wab, pos, b1, w2):
    nseq, n_pages = page_table.shape
    _, _, feat, page = cache_t.shape
    nh = n_pages * page // CMP_STRIDE
    cs = lambda a: pl.BlockSpec(a.shape, lambda i, pt: (0,) * a.ndim, pipeline_mode=pl.Buffered(1))
    return pl.pallas_call(
        functools.partial(_compress_sample_body, n_pages=n_pages),
        grid_spec=pltpu.PrefetchScalarGridSpec(
            num_scalar_prefetch=1, grid=(nseq,),
            in_specs=[pl.BlockSpec(memory_space=pl.ANY)] + [cs(a) for a in (wab, pos, b1, w2)],
            out_specs=pl.BlockSpec((1, nh, w2.shape[1]), lambda i, pt: (i, 0, 0)),
            scratch_shapes=[pltpu.VMEM((2, n_pages, 2, feat, page), F32), pltpu.VMEM((2, n_pages * page, feat), F32),
                            pltpu.SemaphoreType.DMA((2,))]),
        out_shape=jax.ShapeDtypeStruct((nseq, nh, w2.shape[1]), BF16),
        compiler_params=_params(1), name="compress_sample",
    )(page_table, cache_t, wab, pos, b1, w2)


def _masked_softmax(s, mask):
    sm = jnp.where(mask, s, NEG)
    m = jnp.max(sm, axis=-1, keepdims=True)
    p = jnp.where(mask, jnp.exp(sm - m), 0.0)
    l = jnp.sum(p, axis=-1, keepdims=True)
    return p * (1.0 / jnp.maximum(l, 1e-30))


def _split_bf16(x):
    hi = x.astype(BF16)
    return hi, (x - hi.astype(F32)).astype(BF16)


def _alibi_slopes(head_idx):
    out = jnp.zeros(head_idx.shape, F32)
    for h in range(N_HEADS):
        out = jnp.where(head_idx == h, SLOPES[h], out)
    return out


def _nsa_prompt_body(q_ref, gate_ref, ckv_ref, ksa_ref, vsd_ref, kwp_ref, vwd_ref, ov_ref, go_ref,
                     o_ref, s_ref, qsel_ref, mrun_ref, mb_ref, lrun_ref, acc_ref, *, tq, tk, nwin):
    t0 = pl.program_id(1) * tq
    rows = GQA * tq
    lane = lax.broadcasted_iota(jnp.int32, (tq, LANES), 1)
    lo = lane < HALF
    t_rel = lax.broadcasted_iota(jnp.int32, (rows, 1), 0) & (tq - 1)
    cur = (t0 + lax.broadcasted_iota(jnp.int32, (tq, 1), 0)) >> _log2(SEL_BLOCK)
    sub = lax.broadcasted_iota(jnp.int32, (8, tq), 0)
    gates = gate_ref[...]
    nh = ckv_ref.shape[1]
    c_rel = lax.broadcasted_iota(jnp.int32, (1, nh), 1) * CMP_STRIDE + (CMP_LEN - 1) - t0
    n_full = t0 // tk
    w_start = pl.multiple_of(jnp.maximum(t0 - WINDOW, 0), tq)
    w_rel = (w_start - t0) + lax.broadcasted_iota(jnp.int32, (1, nwin), 1)
    cols = [slice(k * LANES, (k + 1) * LANES) for k in range(N_KV_HEADS)]

    def add_alibi(s, kvh, rel):
        relf = rel.astype(F32)
        return jnp.concatenate([s[g * tq:(g + 1) * tq] + SLOPES[kvh * GQA + g] * relf for g in range(GQA)], axis=0)

    o_cmp, o_win = [], []
    for kvh in range(N_KV_HEADS):
        qs = [q_ref[:, (kvh * GQA + g) * LANES:(kvh * GQA + g + 1) * LANES] for g in range(GQA)]
        qall = jnp.concatenate(qs, axis=0)
        s = add_alibi(_dot_nt(qall, ckv_ref[0, :, cols[kvh]]), kvh, c_rel)
        pn = _masked_softmax(s, c_rel <= t_rel)
        o_cmp.append(_dot(pn.astype(BF16), ckv_ref[0, :, 2 * LANES + kvh * LANES:2 * LANES + (kvh + 1) * LANES]))
        psum = pn[0:tq] + pn[tq:2 * tq] + pn[2 * tq:3 * tq] + pn[3 * tq:4 * tq]
        hi, lw = _split_bf16(psum)
        imp = _dot(hi, ov_ref[...]) + _dot(lw, ov_ref[...])
        j = lane - HALF
        forced = (j == 0) | (j == cur) | (j == cur - 1)
        score = jnp.where(forced, BIG, jnp.where(j <= cur, imp, -BIG))
        sc_t = score.T[HALF:, :]
        cnt = [jnp.zeros((8, tq), F32) for _ in range(HALF // 8)]
        for jp in range(HALF):
            r = sc_t[jp:jp + 1, :]
            for v in range(HALF // 8):
                blk = sc_t[8 * v:8 * v + 8, :]
                if 8 * v > jp:
                    beats = _ones_where(r >= blk)
                elif 8 * v + 7 <= jp:
                    beats = _ones_where(r > blk)
                else:
                    beats = _ones_where(r > blk) + jnp.where(r == blk, _ones_where(sub > jp - 8 * v), 0.0)
                cnt[v] = cnt[v] + beats
        bias_t = jnp.where(jnp.concatenate(cnt, axis=0) < float(N_SEL), 0.0, NEG)
        selbias = jnp.concatenate([jnp.zeros((HALF, tq), F32), bias_t], axis=0).T.astype(BF16)
        qsel_ref[kvh] = jnp.concatenate([jnp.where(lo, q, selbias) for q in qs], axis=0)
        s = add_alibi(_dot_nt(qall, kwp_ref[pl.ds(w_start, nwin), cols[kvh]]), kvh, w_rel)
        pw = _masked_softmax(s, (w_rel <= t_rel) & (w_rel >= t_rel - WINDOW))
        o_win.append(_dot(pw.astype(BF16), vwd_ref[pl.ds(w_start, nwin), cols[kvh]]))
        mrun_ref[kvh] = jnp.full((rows, LANES), NEG, F32)
        lrun_ref[kvh] = jnp.zeros((rows, LANES), F32)
        acc_ref[kvh] = jnp.zeros((rows, LANES), F32)

    def scores(jt, causal):
        k0 = pl.multiple_of(jt * tk, tk)
        k_rel = (k0 - t0) + lax.broadcasted_iota(jnp.int32, (1, tk), 1)
        for kvh in range(N_KV_HEADS):
            s = add_alibi(_dot_nt(qsel_ref[kvh], ksa_ref[pl.ds(k0, tk), cols[kvh]]), kvh, k_rel)
            if causal:
                s = jnp.where(k_rel <= t_rel, s, NEG)
            s_ref[kvh, jt] = s
            mt = s[:, :LANES]
            for c in range(1, tk // LANES):
                mt = jnp.maximum(mt, s[:, c * LANES:(c + 1) * LANES])
            mrun_ref[kvh] = jnp.maximum(mrun_ref[kvh], mt)

    def full_tile(jt, carry):
        scores(jt, False)
        return carry

    lax.fori_loop(0, n_full, full_tile, 0)
    scores(n_full, True)
    for kvh in range(N_KV_HEADS):
        mb_ref[kvh] = jnp.broadcast_to(jnp.max(mrun_ref[kvh], axis=-1, keepdims=True), (rows, tk))

    def weigh(jt, carry):
        k0 = pl.multiple_of(jt * tk, tk)
        for kvh in range(N_KV_HEADS):
            p = jnp.exp(s_ref[kvh, jt] - mb_ref[kvh])
            pt = p[:, :LANES]
            for c in range(1, tk // LANES):
                pt = pt + p[:, c * LANES:(c + 1) * LANES]
            lrun_ref[kvh] = lrun_ref[kvh] + pt
            acc_ref[kvh] = acc_ref[kvh] + _dot(p.astype(BF16), vsd_ref[pl.ds(k0, tk), cols[kvh]])
        return carry

    lax.fori_loop(0, n_full + 1, weigh, 0)
    head_out = []
    for kvh in range(N_KV_HEADS):
        o_slc = acc_ref[kvh] * (1.0 / jnp.sum(lrun_ref[kvh], axis=-1, keepdims=True))
        for g in range(GQA):
            h = kvh * GQA + g
            r = slice(g * tq, (g + 1) * tq)
            head_out.append(gates[:, h:h + 1] * o_cmp[kvh][r] + gates[:, N_HEADS + h:N_HEADS + h + 1] * o_slc[r]
                            + gates[:, 2 * N_HEADS + h:2 * N_HEADS + h + 1] * o_win[kvh][r])
    o = jnp.concatenate([jnp.where(lo, head_out[2 * i], head_out[2 * i + 1]) for i in range(N_HEADS // 2)], axis=1)
    o_ref[...] = _rms_norm(o, go_ref[...]).astype(BF16)


def _nsa_prompt(qpad, gate, ckv, ksa, vsd, kwp, vwd, ov, go, nb, seq, tq=128, tk=256):
    assert seq % tk == 0 and tk % tq == 0 and tq == LANES and seq >= WINDOW + tq
    nwin = WINDOW + tq
    nq = seq // tq
    qrow = lambda w: pl.BlockSpec((tq, w), lambda b, i: (b * nq + i, 0))
    seqblk = lambda w: pl.BlockSpec((seq, w), lambda b, i: (b, 0))
    cs = lambda a: pl.BlockSpec(a.shape, lambda b, i: (0,) * a.ndim, pipeline_mode=pl.Buffered(1))
    rows = GQA * tq
    acc = pltpu.VMEM((N_KV_HEADS, rows, LANES), F32)
    return pl.pallas_call(
        functools.partial(_nsa_prompt_body, tq=tq, tk=tk, nwin=nwin),
        grid=(nb, nq),
        in_specs=[qrow(qpad.shape[1]), qrow(LANES), pl.BlockSpec((1,) + ckv.shape[1:], lambda b, i: (b, 0, 0)),
                  seqblk(2 * LANES), seqblk(2 * LANES), seqblk(2 * LANES), seqblk(2 * LANES), cs(ov), cs(go)],
        out_specs=qrow(ATTN_W),
        out_shape=jax.ShapeDtypeStruct((nb * seq, ATTN_W), BF16),
        scratch_shapes=[pltpu.VMEM((N_KV_HEADS, seq // tk, rows, tk), F32), pltpu.VMEM((N_KV_HEADS, rows, LANES), BF16),
                        acc, pltpu.VMEM((N_KV_HEADS, rows, tk), F32), acc, acc],
        compiler_params=_params(2), name="nsa_prompt",
    )(qpad, gate, ckv, ksa, vsd, kwp, vwd, ov, go)


T8 = 8


def _nsa_sample_body(pt_ref, cache_ref, q_ref, gate_ref, ckv_ref, win_ref, new_ref, e_ref, ov_ref, go_ref,
                     o_ref, buf_ref, sem_ref, *, n_pages, past, n_new, n_cmp):
    def make_copies(seq_idx, slot):
        return [pltpu.make_async_copy(cache_ref.at[pt_ref[seq_idx, j]],
                                      buf_ref.at[slot, :, :, pl.ds(j * PAGE_SIZE, PAGE_SIZE)], sem_ref.at[slot])
                for j in range(n_pages)]

    slot = _gather_step(make_copies)
    rows = N_KV_HEADS * GQA * T8
    ridx = lax.broadcasted_iota(jnp.int32, (rows, 1), 0)
    t_new = ridx & (n_new - 1)
    slope = _alibi_slopes(ridx >> _log2(T8))
    q = q_ref[0]
    new = new_ref[0]

    def tail(x):
        return jnp.concatenate([x, jnp.zeros((LANES - T8, x.shape[1]), F32)], axis=0).astype(BF16)

    def attend(kt, vt, knew, vnew, extra, n_past):
        rel = lax.broadcasted_iota(jnp.int32, (1, n_past + LANES), 1) - n_past
        s = jnp.concatenate([_dot(q, kt) if extra is None else _dot(*extra), _dot_nt(q, knew)], axis=1)
        s = s + slope * rel.astype(F32)
        p = _masked_softmax(s, (rel <= t_new) & (rel >= t_new - WINDOW) & (rel + past >= 0)
                            if extra is None else rel <= t_new)
        return _dot_nt(p[:, :n_past].astype(BF16), vt) + _dot(p[:, n_past:].astype(BF16), vnew)

    nh = ckv_ref.shape[1]
    n_i = lax.broadcasted_iota(jnp.int32, (1, nh), 1)
    dist = (past + t_new) - (n_i * CMP_STRIDE + (CMP_LEN - 1))
    ckv = ckv_ref[0]
    s = _dot_nt(q, ckv[:, :LANES]) - slope * dist.astype(F32)
    pn = _masked_softmax(s, (dist >= 0) & (n_i < n_cmp))
    o_cmp = _dot(pn.astype(BF16), ckv[:, LANES:])
    psum = jnp.concatenate(
        [sum(pn[(k * GQA + g) * T8:(k * GQA + g + 1) * T8] for g in range(GQA)) for k in range(N_KV_HEADS)], axis=0)
    hi, lw = _split_bf16(psum)
    imp = _dot(hi, ov_ref[...]) + _dot(lw, ov_ref[...])
    n_blk = past // SEL_BLOCK
    lane = lax.broadcasted_iota(jnp.int32, (N_KV_HEADS * T8, LANES), 1)
    forced = (lane == 0) | (lane == n_blk - 1)
    score = jnp.where(forced, BIG, jnp.where(lane < n_blk, imp, -BIG))
    cnt = jnp.zeros(score.shape, F32)
    for jp in range(n_blk):
        r = score[:, jp:jp + 1]
        cnt = cnt + _ones_where(r > score) + jnp.where(r == score, _ones_where(lane > jp), 0.0)
    bias = jnp.where(cnt < float(N_SEL - 1), 0.0, NEG).astype(BF16)
    selbias = jnp.concatenate([bias[k * T8:(k + 1) * T8] for k in range(N_KV_HEADS) for _ in range(GQA)], axis=0)
    kt = buf_ref[slot, 0].astype(BF16)
    vt = buf_ref[slot, 1].astype(BF16)
    q_aug = jnp.concatenate([q, selbias], axis=1)
    k_aug = jnp.concatenate([kt, e_ref[...]], axis=0)
    o_slc = attend(kt, vt, tail(new[:, 2 * KV_W:3 * KV_W]), tail(new[:, 3 * KV_W:4 * KV_W]), (q_aug, k_aug), past)
    nw = win_ref.shape[3]
    o_win = attend(win_ref[0, 0].astype(BF16), win_ref[0, 1].astype(BF16),
                   tail(new[:, 4 * KV_W:5 * KV_W]), tail(new[:, 5 * KV_W:6 * KV_W]), None, nw)
    gates = gate_ref[0]
    o = gates[:, 0:1] * o_cmp + gates[:, 1:2] * o_slc + gates[:, 2:3] * o_win
    o_r = pltpu.roll(o, HALF, axis=1)
    lo = lax.broadcasted_iota(jnp.int32, (T8, LANES), 1) < HALF
    groups = []
    for k in range(N_KV_HEADS):
        for i in range(GQA // 2):
            r_even = slice((k * GQA + 2 * i) * T8, (k * GQA + 2 * i + 1) * T8)
            r_odd = slice((k * GQA + 2 * i + 1) * T8, (k * GQA + 2 * i + 2) * T8)
            groups.append(jnp.where(lo, o[r_even], o_r[r_odd]) if k == 0 else jnp.where(lo, o_r[r_even], o[r_odd]))
    o_ref[0] = _rms_norm(jnp.concatenate(groups, axis=1), go_ref[...]).astype(BF16)


def _nsa_sample(page_table, cache_t, q_bd, gate_rows, ckv, win_t, new8, e_mat, ov, go, past, n_new):
    nseq, n_pages = page_table.shape
    assert past == n_pages * PAGE_SIZE and past % SEL_BLOCK == 0 and past // SEL_BLOCK <= LANES
    assert n_new <= T8 and T8 % n_new == 0 and past % CMP_STRIDE == 0
    per_seq = lambda a: pl.BlockSpec((1,) + a.shape[1:], lambda i, pt: (i,) + (0,) * (a.ndim - 1))
    cs = lambda a: pl.BlockSpec(a.shape, lambda i, pt: (0,) * a.ndim, pipeline_mode=pl.Buffered(1))
    return pl.pallas_call(
        functools.partial(_nsa_sample_body, n_pages=n_pages, past=past, n_new=n_new,
                          n_cmp=past // CMP_STRIDE - 1),
        grid_spec=pltpu.PrefetchScalarGridSpec(
            num_scalar_prefetch=1, grid=(nseq,),
            in_specs=[pl.BlockSpec(memory_space=pl.ANY), per_seq(q_bd), per_seq(gate_rows), per_seq(ckv),
                      per_seq(win_t), per_seq(new8), cs(e_mat), cs(ov), cs(go)],
            out_specs=pl.BlockSpec((1, T8, ATTN_W), lambda i, pt: (i, 0, 0)),
            scratch_shapes=[pltpu.VMEM((2, 2, KV_W, past), F32), pltpu.SemaphoreType.DMA((2,))]),
        out_shape=jax.ShapeDtypeStruct((nseq, T8, ATTN_W), BF16),
        compiler_params=_params(1), name="nsa_sample",
    )(page_table, cache_t, q_bd, gate_rows, ckv, win_t, new8, e_mat, ov, go)


def _overlap(n_rows, n_cmp, lane0):
    n = np.arange(n_rows)[:, None]
    j = np.arange(LANES)[None, :] - lane0
    c_end = n * CMP_STRIDE + (CMP_LEN - 1)
    ov = (j * SEL_BLOCK <= c_end) & (j * SEL_BLOCK + SEL_BLOCK > c_end + 1 - CMP_LEN) & (n < n_cmp) & (j >= 0)
    return jnp.asarray(ov, BF16)


def _compress_weights(k_pos, k_w1, k_b1, k_w2, v_pos, v_w1, v_b1, v_w2):
    eye = jnp.eye(4, dtype=F32)
    feat = 4 * HEAD_DIM
    w1 = jnp.stack([k_w1, k_w1, v_w1, v_w1])
    half = lambda a: jnp.einsum("cldh,ce->lcdeh", a, eye).reshape(CMP_STRIDE, feat, 4 * CMP_HIDDEN)
    wab = jnp.concatenate([half(w1[:, :CMP_STRIDE]), half(w1[:, CMP_STRIDE:])], axis=2).astype(BF16)
    pos = jnp.stack([k_pos, k_pos, v_pos, v_pos])
    flat = lambda a: jnp.broadcast_to(jnp.transpose(a, (1, 0, 2)).reshape(CMP_STRIDE, 1, feat), (CMP_STRIDE, 8, feat))
    pos16 = jnp.concatenate([flat(pos[:, :CMP_STRIDE]), flat(pos[:, CMP_STRIDE:])], axis=1).astype(BF16)
    b1 = jnp.concatenate([k_b1, k_b1, v_b1, v_b1]).reshape(1, -1)
    z = jnp.zeros((CMP_HIDDEN, HEAD_DIM), F32)
    w2p = jnp.concatenate([
        jnp.concatenate([k_w2, z, z, z, z, z, z, z], axis=1), jnp.concatenate([z, z, k_w2, z, z, z, z, z], axis=1),
        jnp.concatenate([z, z, z, z, v_w2, v_w2, z, z], axis=1), jnp.concatenate([z, z, z, z, z, z, v_w2, v_w2], axis=1),
    ], axis=0).astype(BF16)
    w2s = jnp.concatenate([
        jnp.concatenate([k_w2, z, z, z], axis=1), jnp.concatenate([z, k_w2, z, z], axis=1),
        jnp.concatenate([z, z, v_w2, z], axis=1), jnp.concatenate([z, z, z, v_w2], axis=1)], axis=0).astype(BF16)
    return wab, pos16, b1, w2p, w2s


def _proj_weight(w_in):
    d = w_in.shape[0]
    g0 = ATTN_W + 6 * KV_W
    wg = jnp.concatenate([w_in[:, g0:g0 + 3 * N_HEADS], jnp.zeros((d, LANES - 3 * N_HEADS), F32)], axis=1)
    return jnp.concatenate([w_in[:, :ATTN_W] * HEAD_DIM ** -0.5, w_in[:, ATTN_W:g0], wg,
                            w_in[:, g0 + 3 * N_HEADS:]], axis=1).astype(BF16)


def _feature_major(a):
    lead = a.shape[:-4]
    n = len(lead)
    a = jnp.transpose(a, tuple(range(n)) + (n + 1, n + 2, n + 3, n))
    return a.reshape(lead + (2, KV_W, a.shape[-1]))


def _token_major(a, nb):
    return jnp.transpose(a.reshape(nb, 2, N_KV_HEADS, HEAD_DIM, a.shape[-1]), (0, 4, 1, 2, 3))


def kernel(x_prompt, x_sample, cache_cmp_kv, cache_slc_kv, state_win_kv, page_table, ln1_g, ln1_b, ffn1_w_in, ffn1_w_out, w_in, cmp_k_pos, cmp_k_w1, cmp_k_b1, cmp_k_w2, cmp_v_pos, cmp_v_w1, cmp_v_b1, cmp_v_w2, sgu_ln_g, sgu_ln_b, sgu_w, sgu_b, out_norm_g, w_out, ln2_g, ln2_b, ffn2_w_in, ffn2_w_out, ln3_g, ln3_b):
    depth = w_in.shape[0]
    nb, seq, d = x_prompt.shape
    ns, n_new, _ = x_sample.shape
    n_pages = page_table.shape[1]
    past = n_pages * PAGE_SIZE
    d_ff = ffn1_w_out.shape[1]
    alpha = (2.0 * depth) ** 0.25
    row = lambda a: a.reshape(1, -1)
    hp = x_prompt.reshape(nb * seq, d)
    hs = x_sample.reshape(ns * n_new, d)
    kv5 = (N_KV_HEADS, HEAD_DIM)
    outs = [[] for _ in range(7)]
    ov_p = _overlap(seq // CMP_STRIDE, seq // CMP_STRIDE - 1, HALF)
    ov_s = _overlap(past // CMP_STRIDE, past // CMP_STRIDE - 1, 0)
    e_mat = jnp.asarray(np.arange(LANES)[:, None] == np.arange(past)[None, :] // SEL_BLOCK, BF16)
    for l in range(depth):
        f1 = (ffn1_w_in[l, :, :d_ff].astype(BF16), ffn1_w_in[l, :, d_ff:].astype(BF16), ffn1_w_out[l].astype(BF16),
              row(ln1_g[l]), row(ln1_b[l]))
        f2 = (ffn2_w_in[l, :, :d_ff].astype(BF16), ffn2_w_in[l, :, d_ff:].astype(BF16), ffn2_w_out[l].astype(BF16),
              row(ln3_g[l]), row(ln3_b[l]))
        wp = _proj_weight(w_in[l])
        wab, pos16, b1, w2p, w2s = _compress_weights(cmp_k_pos[l], cmp_k_w1[l], cmp_k_b1[l], cmp_k_w2[l],
                                                     cmp_v_pos[l], cmp_v_w1[l], cmp_v_b1[l], cmp_v_w2[l])
        sg, sb = row(sgu_ln_g[l]), row(sgu_ln_b[l])
        go_a, go_g = row(out_norm_g[l, :ATTN_W]), row(out_norm_g[l, ATTN_W:])
        wm = w_out[l].astype(BF16)
        mix_p = jnp.tril(sgu_w[l]).astype(BF16)
        mb_p = jnp.repeat(sgu_b[l].T, GMLP_DIM, axis=1)
        r = min(n_new, CHUNK)
        mix_s = jax.vmap(lambda w: jnp.kron(jnp.eye(CHUNK // r, dtype=F32), w))(jnp.tril(sgu_w[l, :, :r, :r])).astype(BF16)
        mb_s = jnp.tile(jnp.repeat(sgu_b[l, :, :r].T, GMLP_DIM, axis=1), (CHUNK // r, 1))

        hp = _ffn_ln(hp, *f1, alpha)
        (qpad, cmpk_p, cmpv_p, _, _, cmp_t, slc_t, win_t, ksa, vsd, kwp, vwd, gate, ogn, _) = _proj(
            hp, wp, sg, sb, mix_p, mb_p, go_g, seq)
        ckv = _compress_prompt(cmpk_p, cmpv_p, nb, wab, pos16, b1, w2p)
        oan = _nsa_prompt(qpad, gate, ckv, ksa, vsd, kwp, vwd, ov_p, go_a, nb, seq)
        hp = _ffn_ln(hp, *f2, alpha, merge_args=(oan, ogn, wm, row(ln2_g[l]), row(ln2_b[l])))
        outs[0].append(_token_major(cmp_t, nb))
        outs[2].append(_token_major(slc_t, nb))
        outs[4].append(_token_major(win_t[:, :, seq - min(WINDOW, seq):], nb))

        hs = _ffn_ln(hs, *f1, alpha)
        (qpad, cmpk_s, cmpv_s, slc_s, win_s, _, _, _, _, _, _, _, gate, ogn, v_s) = _proj(
            hs, wp, sg, sb, mix_s, mb_s, go_g, ns * n_new)
        cmp_s = jnp.concatenate([cmpk_s, cmpv_s], axis=1)
        ckv = _compress_sample(page_table, _feature_major(cache_cmp_kv[l]), wab, pos16, b1, w2s)
        q5 = qpad.reshape(ns, n_new, N_KV_HEADS, GQA, LANES)[..., :HEAD_DIM]
        q5 = jnp.pad(jnp.transpose(q5, (0, 2, 3, 1, 4)), ((0, 0), (0, 0), (0, 0), (0, T8 - n_new), (0, 0)))
        zq = jnp.zeros_like(q5[:, 0])
        q_bd = jnp.stack([jnp.concatenate([q5[:, 0], zq], axis=-1), jnp.concatenate([zq, q5[:, 1]], axis=-1)],
                         axis=1).reshape(ns, N_KV_HEADS * GQA * T8, LANES)
        g5 = gate[:, :3 * N_HEADS].reshape(ns, n_new, 3, N_KV_HEADS, GQA)
        g5 = jnp.pad(jnp.transpose(g5, (0, 3, 4, 1, 2)), ((0, 0), (0, 0), (0, 0), (0, T8 - n_new), (0, 0)))
        gate_rows = g5.reshape(ns, N_KV_HEADS * GQA * T8, 3)
        new8 = jnp.pad(jnp.concatenate([cmp_s, slc_s, win_s], axis=1).reshape(ns, n_new, 6 * KV_W),
                       ((0, 0), (0, T8 - n_new), (0, 0)))
        oan = _nsa_sample(page_table, _feature_major(cache_slc_kv[l]), q_bd, gate_rows, ckv,
                          _feature_major(state_win_kv[l]), new8, e_mat, ov_s, go_a, past, n_new)
        oan = oan[:, :n_new].reshape(ns * n_new, ATTN_W)
        hs = _ffn_ln(hs, *f2, alpha, merge_args=(oan, ogn, wm, row(ln2_g[l]), row(ln2_b[l])))
        win_all = jnp.concatenate([state_win_kv[l], win_s.reshape(ns, n_new, 2, *kv5)], axis=1)
        outs[1].append(cmp_s.reshape(ns, n_new, 2, *kv5))
        outs[3].append(slc_s.reshape(ns, n_new, 2, *kv5))
        outs[5].append(win_all[:, -min(WINDOW, past + n_new):])
        outs[6].append(v_s.reshape(ns, n_new, N_GMLP, GMLP_DIM))
    st = [jnp.stack(o, axis=0) for o in outs]
    return (hp.reshape(nb, seq, d), hs.reshape(ns, n_new, d), st[0], st[1], st[2], st[3], st[4], st[5], st[6])
```

```python
import functools

import numpy as np
import jax
import jax.numpy as jnp
from jax import lax
from jax.experimental import pallas as pl
from jax.experimental.pallas import tpu as pltpu

HEAD_DIM = 64
N_HEADS = 8
N_KV_HEADS = 2
GQA = N_HEADS // N_KV_HEADS
ATTN_W = N_HEADS * HEAD_DIM
KV_W = N_KV_HEADS * HEAD_DIM
CMP_LEN = 32
CMP_STRIDE = 16
CMP_HIDDEN = 64
SEL_BLOCK = 64
N_SEL = 16
WINDOW = 512
N_GMLP = 8
GMLP_DIM = 64
GMLP_W = N_GMLP * GMLP_DIM
CHUNK = 128
PAGE_SIZE = 128
LN_EPS = 1e-5
NEG = -1e30
BIG = 1e9

LANES = 128
HALF = LANES // 2
VMEM_LIMIT = 56 * 1024 * 1024
SLOPES = [2.0 ** -(h + 1) for h in range(N_HEADS)]

F32 = jnp.float32
BF16 = jnp.bfloat16

_Q0, _KV0, _G0, _U0, _V0, _PW = 0, ATTN_W, ATTN_W + 6 * KV_W, ATTN_W + 6 * KV_W + LANES, \
    ATTN_W + 6 * KV_W + LANES + GMLP_W, ATTN_W + 6 * KV_W + LANES + 2 * GMLP_W


def _dot(a, b):
    return jnp.dot(a, b, preferred_element_type=F32)


def _dot_nt(a, b):
    return lax.dot_general(a, b, (((1,), (1,)), ((), ())), preferred_element_type=F32)


def _layer_norm(y, g, b):
    mu = jnp.mean(y, axis=-1, keepdims=True)
    d = y - mu
    var = jnp.mean(d * d, axis=-1, keepdims=True)
    return d * lax.rsqrt(var + LN_EPS) * g + b


def _rms_norm(y, g):
    return y * lax.rsqrt(jnp.mean(y * y, axis=-1, keepdims=True) + LN_EPS) * g


def _ones_where(mask):
    return jnp.where(mask, 1.0, 0.0)


def _alibi_lanes(lane, hi, lo):
    return jnp.where(lane == HALF, hi, jnp.where(lane == HALF + 1, lo, 0.0))


def _log2(n):
    assert n > 0 and n & (n - 1) == 0, n
    return n.bit_length() - 1


def _const_spec(shape):
    nd = len(shape)
    return pl.BlockSpec(shape, lambda *_: (0,) * nd, pipeline_mode=pl.Buffered(1))


def _params(n_grid):
    return pltpu.CompilerParams(dimension_semantics=("arbitrary",) * n_grid, vmem_limit_bytes=VMEM_LIMIT)


def _ffn_body(*refs, alpha, merge):
    if merge:
        (h_ref, oa_ref, og_ref, wm_ref, gm_ref, bm_ref,
         wa_ref, wb_ref, wo_ref, g_ref, b_ref, o_ref) = refs
        cat = jnp.concatenate([oa_ref[...], og_ref[...]], axis=1)
        x = _layer_norm(alpha * h_ref[...] + _dot(cat, wm_ref[...]), gm_ref[...], bm_ref[...])
    else:
        x_ref, wa_ref, wb_ref, wo_ref, g_ref, b_ref, o_ref = refs
        x = x_ref[...]
    xb = x.astype(BF16)
    a = _dot(xb, wa_ref[...])
    b = _dot(xb, wb_ref[...])
    hid = ((a * jax.nn.sigmoid(a)) * b).astype(BF16)
    y = alpha * x + 0.5 * _dot(hid, wo_ref[...])
    o_ref[...] = _layer_norm(y, g_ref[...], b_ref[...])


def _ffn_ln(x, wa, wb, wo, g, b, alpha, merge_args=None, tm=512):
    t, d = x.shape
    tm = min(tm, t)
    assert t % tm == 0
    row = lambda w: pl.BlockSpec((tm, w), lambda i: (i, 0))
    ins, specs = [x], [row(d)]
    if merge_args is not None:
        oa, og, wm, gm, bm = merge_args
        ins += [oa, og, wm, gm, bm]
        specs += [row(oa.shape[1]), row(og.shape[1]), _const_spec(wm.shape), _const_spec(gm.shape),
                  _const_spec(bm.shape)]
    ins += [wa, wb, wo, g, b]
    specs += [_const_spec(a.shape) for a in (wa, wb, wo, g, b)]
    return pl.pallas_call(
        functools.partial(_ffn_body, alpha=alpha, merge=merge_args is not None),
        grid=(t // tm,), in_specs=specs, out_specs=row(d),
        out_shape=jax.ShapeDtypeStruct((t, d), F32), compiler_params=_params(1),
        name="ffn_merge_ln" if merge_args is not None else "ffn_ln",
    )(*ins)


def _proj_body(h_ref, w_ref, sg_ref, sb_ref, mix_ref, mb_ref, go_ref,
               qpad_ref, cmpk_ref, cmpv_ref, slc_ref, win_ref, cmpt_ref, slct_ref, wint_ref,
               ksa_ref, vsd_ref, kwp_ref, vwd_ref, gate_ref, ogn_ref, v_ref, *, tm, seq):
    p = _dot(h_ref[...].astype(BF16), w_ref[...])
    lane = lax.broadcasted_iota(jnp.int32, (tm, LANES), 1)
    lo = lane < HALF
    tok = lax.rem(pl.program_id(0) * tm, seq) + lax.broadcasted_iota(jnp.int32, (tm, LANES), 0)
    for i in range(ATTN_W // LANES):
        x = p[:, _Q0 + i * LANES:_Q0 + (i + 1) * LANES]
        xr = pltpu.roll(x, HALF, axis=1)
        qpad_ref[:, (2 * i) * LANES:(2 * i + 1) * LANES] = jnp.where(
            lo, x, _alibi_lanes(lane, LANES * SLOPES[2 * i], SLOPES[2 * i])).astype(BF16)
        qpad_ref[:, (2 * i + 1) * LANES:(2 * i + 2) * LANES] = jnp.where(
            lo, xr, _alibi_lanes(lane, LANES * SLOPES[2 * i + 1], SLOPES[2 * i + 1])).astype(BF16)
    for s, feat_ref in enumerate((cmpt_ref, slct_ref, wint_ref)):
        feat_ref[0] = p[:, _KV0 + 2 * s * KV_W:_KV0 + 2 * (s + 1) * KV_W].T
    cmpk_ref[...] = p[:, _KV0:_KV0 + KV_W]
    cmpv_ref[...] = p[:, _KV0 + KV_W:_KV0 + 2 * KV_W]
    slc_ref[...] = p[:, _KV0 + 2 * KV_W:_KV0 + 4 * KV_W]
    win_ref[...] = p[:, _KV0 + 4 * KV_W:_KV0 + 6 * KV_W]
    onehot = _ones_where(lane - HALF == (tok >> _log2(SEL_BLOCK)))
    ks = p[:, _KV0 + 2 * KV_W:_KV0 + 3 * KV_W]
    ksr = pltpu.roll(ks, HALF, axis=1)
    ksa_ref[:, :LANES] = jnp.where(lo, ks, onehot).astype(BF16)
    ksa_ref[:, LANES:] = jnp.where(lo, ksr, onehot).astype(BF16)
    vs = p[:, _KV0 + 3 * KV_W:_KV0 + 4 * KV_W]
    vsr = pltpu.roll(vs, HALF, axis=1)
    vsd_ref[:, :LANES] = jnp.where(lo, vs, vsr).astype(BF16)
    vsd_ref[:, LANES:] = jnp.where(lo, vsr, vs).astype(BF16)
    kw = p[:, _KV0 + 4 * KV_W:_KV0 + 5 * KV_W]
    kwr = pltpu.roll(kw, HALF, axis=1)
    pos_pair = _alibi_lanes(lane, (tok >> _log2(LANES)).astype(F32), (tok & (LANES - 1)).astype(F32))
    kwp_ref[:, :LANES] = jnp.where(lo, kw, pos_pair).astype(BF16)
    kwp_ref[:, LANES:] = jnp.where(lo, kwr, pos_pair).astype(BF16)
    vw = p[:, _KV0 + 5 * KV_W:_KV0 + 6 * KV_W]
    vwr = pltpu.roll(vw, HALF, axis=1)
    vwd_ref[:, :LANES] = jnp.where(lo, vw, vwr).astype(BF16)
    vwd_ref[:, LANES:] = jnp.where(lo, vwr, vw).astype(BF16)
    gate_ref[...] = jax.nn.sigmoid(p[:, _G0:_G0 + LANES])
    u = jax.nn.gelu(p[:, _U0:_U0 + GMLP_W])
    v = _layer_norm(jax.nn.gelu(p[:, _V0:_V0 + GMLP_W]), sg_ref[...], sb_ref[...])
    v_ref[...] = v
    vb = v.astype(BF16)
    lo_c = lax.broadcasted_iota(jnp.int32, (CHUNK, LANES), 1) < HALF
    for c in range(tm // CHUNK):
        rows = slice(c * CHUNK, (c + 1) * CHUNK)
        zs = []
        for i in range(GMLP_W // LANES):
            vp = vb[rows, i * LANES:(i + 1) * LANES]
            zs.append(jnp.where(lo_c, _dot(mix_ref[2 * i], vp), _dot(mix_ref[2 * i + 1], vp)))
        z = jnp.concatenate(zs, axis=1) + mb_ref[...]
        ogn_ref[rows, :] = _rms_norm(u[rows, :] * z, go_ref[...]).astype(BF16)


def _proj(h, w, sg, sb, mix, mb, go, seq, tm=512):
    t, d = h.shape
    tm = min(tm, t)
    assert t % tm == 0 and tm % CHUNK == 0 and seq % tm == 0 and t % seq == 0 and seq // SEL_BLOCK <= HALF
    nq = seq // tm
    row = lambda width: pl.BlockSpec((tm, width), lambda i: (i, 0))
    feat = pl.BlockSpec((1, 2 * KV_W, tm), lambda i: (i // nq, 0, i % nq))
    tok_major = [(2 * ATTN_W, BF16), (KV_W, F32), (KV_W, F32), (2 * KV_W, F32), (2 * KV_W, F32)]
    rest = [(2 * LANES, BF16), (2 * LANES, BF16), (2 * LANES, BF16), (2 * LANES, BF16), (LANES, F32),
            (GMLP_W, BF16), (GMLP_W, F32)]
    sds = jax.ShapeDtypeStruct
    return pl.pallas_call(
        functools.partial(_proj_body, tm=tm, seq=seq),
        grid=(t // tm,),
        in_specs=[row(d)] + [_const_spec(a.shape) for a in (w, sg, sb, mix, mb, go)],
        out_specs=[row(wd) for wd, _ in tok_major] + [feat] * 3 + [row(wd) for wd, _ in rest],
        out_shape=[sds((t, wd), dt) for wd, dt in tok_major] + [sds((t // seq, 2 * KV_W, seq), F32)] * 3
        + [sds((t, wd), dt) for wd, dt in rest],
        compiler_params=_params(1), name="proj",
    )(h, w, sg, sb, mix, mb, go)


def _compress_math(rows_at, nh, wab_ref, pos_ref, b1_ref, w2_ref):
    hw = b1_ref.shape[1]
    ab = None
    for l in range(CMP_STRIDE):
        lhs = jnp.concatenate([rows_at(l).astype(BF16), pos_ref[l]], axis=0)
        d = _dot(lhs, wab_ref[l])
        ab = d if ab is None else ab + d
    c = ab[nh:nh + 1, :hw] + ab[nh + 8:nh + 9, hw:] + b1_ref[...]
    hid = ab[:nh, :hw] + pltpu.roll(ab[:nh, hw:], nh - 1, axis=0) + c
    return _dot(jax.nn.gelu(hid).astype(BF16), w2_ref[...])


def _strided_rows(k_ref, v_ref, nh):
    return lambda l: jnp.concatenate([k_ref[pl.ds(l, nh, stride=CMP_STRIDE), :],
                                      v_ref[pl.ds(l, nh, stride=CMP_STRIDE), :]], axis=1)


def _compress_body(xk_ref, xv_ref, wab_ref, pos_ref, b1_ref, w2_ref, o_ref):
    nh = o_ref.shape[1]
    out = _compress_math(_strided_rows(xk_ref, xv_ref, nh), nh, wab_ref, pos_ref, b1_ref, w2_ref)
    lane = lax.broadcasted_iota(jnp.int32, (nh, LANES), 1)
    c_end = lax.broadcasted_iota(jnp.int32, (nh, LANES), 0) * CMP_STRIDE + (CMP_LEN - 1)
    pair = _alibi_lanes(lane, (c_end >> _log2(LANES)).astype(F32), (c_end & (LANES - 1)).astype(F32))
    for k in range(N_KV_HEADS):
        o_ref[0, :, k * LANES:(k + 1) * LANES] = (out[:, k * LANES:(k + 1) * LANES] + pair).astype(o_ref.dtype)
    o_ref[0, :, N_KV_HEADS * LANES:] = out[:, N_KV_HEADS * LANES:].astype(o_ref.dtype)


def _compress_prompt(xk, xv, nb, wab, pos, b1, w2):
    seq = xk.shape[0] // nb
    nh = seq // CMP_STRIDE
    return pl.pallas_call(
        _compress_body, grid=(nb,),
        in_specs=[pl.BlockSpec((seq, KV_W), lambda i: (i, 0))] * 2 + [_const_spec(a.shape) for a in (wab, pos, b1, w2)],
        out_specs=pl.BlockSpec((1, nh, w2.shape[1]), lambda i: (i, 0, 0)),
        out_shape=jax.ShapeDtypeStruct((nb, nh, w2.shape[1]), BF16),
        compiler_params=_params(1), name="compress_prompt",
    )(xk, xv, wab, pos, b1, w2)


def _gather_step(make_copies):
    i, n = pl.program_id(0), pl.num_programs(0)
    slot = lax.rem(i, 2)

    @pl.when(i == 0)
    def _():
        for cp in make_copies(i, slot):
            cp.start()

    @pl.when(i + 1 < n)
    def _():
        for cp in make_copies(i + 1, 1 - slot):
            cp.start()

    for cp in make_copies(i, slot):
        cp.wait()
    return slot


def _compress_sample_body(pt_ref, cache_ref, perm_ref, wab_ref, pos_ref, b1_ref, w2_ref, o_ref, buf_ref, xs_ref,
                          sem_ref, *, n_pages):
    def make_copies(seq_idx, slot):
        return [pltpu.make_async_copy(cache_ref.at[pt_ref[seq_idx, j]], buf_ref.at[slot, j], sem_ref.at[slot])
                for j in range(n_pages)]

    slot = _gather_step(make_copies)
    per_page = PAGE_SIZE // CMP_STRIDE

    def to_token_major(j, carry):
        r = pl.ds(pl.multiple_of(j * per_page, per_page), per_page)
        for c in range(2):
            z = _dot_nt(perm_ref[...], buf_ref[slot, j, c].astype(BF16))
            for l in range(CMP_STRIDE):
                xs_ref[c, l, r, :] = z[l * per_page:(l + 1) * per_page, :]
        return carry

    lax.fori_loop(0, n_pages, to_token_major, 0)
    nh = o_ref.shape[1]
    rows_at = lambda l: jnp.concatenate([xs_ref[0, l], xs_ref[1, l]], axis=1)
    o_ref[0] = _compress_math(rows_at, nh, wab_ref, pos_ref, b1_ref, w2_ref).astype(o_ref.dtype)


def _compress_sample(page_table, cache_t, wab, pos, b1, w2):
    nseq, n_pages = page_table.shape
    _, _, feat, page = cache_t.shape
    nh = n_pages * page // CMP_STRIDE
    cs = lambda a: pl.BlockSpec(a.shape, lambda i, pt: (0,) * a.ndim, pipeline_mode=pl.Buffered(1))
    tok = np.arange(page)
    perm = jnp.asarray((tok % CMP_STRIDE * (page // CMP_STRIDE) + tok // CMP_STRIDE)[None, :] == tok[:, None], BF16)
    return pl.pallas_call(
        functools.partial(_compress_sample_body, n_pages=n_pages),
        grid_spec=pltpu.PrefetchScalarGridSpec(
            num_scalar_prefetch=1, grid=(nseq,),
            in_specs=[pl.BlockSpec(memory_space=pl.ANY)] + [cs(a) for a in (perm, wab, pos, b1, w2)],
            out_specs=pl.BlockSpec((1, nh, w2.shape[1]), lambda i, pt: (i, 0, 0)),
            scratch_shapes=[pltpu.VMEM((2, n_pages, 2, feat, page), F32), pltpu.VMEM((2, CMP_STRIDE, nh, feat), F32),
                            pltpu.SemaphoreType.DMA((2,))]),
        out_shape=jax.ShapeDtypeStruct((nseq, nh, w2.shape[1]), BF16),
        compiler_params=_params(1), name="compress_sample",
    )(page_table, cache_t, perm, wab, pos, b1, w2)


def _masked_softmax(s, mask):
    sm = jnp.where(mask, s, NEG)
    m = jnp.max(sm, axis=-1, keepdims=True)
    p = jnp.where(mask, jnp.exp(sm - m), 0.0)
    l = jnp.sum(p, axis=-1, keepdims=True)
    return p * (1.0 / jnp.maximum(l, 1e-30))


def _split_bf16(x):
    hi = x.astype(BF16)
    return hi, (x - hi.astype(F32)).astype(BF16)


def _alibi_slopes(head_idx):
    out = jnp.zeros(head_idx.shape, F32)
    for h in range(N_HEADS):
        out = jnp.where(head_idx == h, SLOPES[h], out)
    return out


def _nsa_prompt_body(q_ref, gate_ref, ckv_ref, ksa_ref, vsd_ref, kwp_ref, vwd_ref, ov_ref, go_ref,
                     o_ref, s_ref, qsel_ref, mrun_ref, mb_ref, lrun_ref, acc_ref, tile_ref, *, tq, tk, nwin):
    n_tiles = s_ref.shape[1]
    t0 = pl.program_id(1) * tq
    rows = GQA * tq
    lane = lax.broadcasted_iota(jnp.int32, (tq, LANES), 1)
    lo = lane < HALF
    t_rel = lax.broadcasted_iota(jnp.int32, (rows, 1), 0) & (tq - 1)
    cur = (t0 + lax.broadcasted_iota(jnp.int32, (tq, 1), 0)) >> _log2(SEL_BLOCK)
    sub = lax.broadcasted_iota(jnp.int32, (8, tq), 0)
    gates = gate_ref[...]
    nh = ckv_ref.shape[1]
    c_rel = lax.broadcasted_iota(jnp.int32, (1, nh), 1) * CMP_STRIDE + (CMP_LEN - 1) - t0
    n_full = t0 // tk
    w_start = pl.multiple_of(jnp.maximum(t0 - WINDOW, 0), tq)
    w_rel = (w_start - t0) + lax.broadcasted_iota(jnp.int32, (1, nwin), 1)
    t_q = lax.broadcasted_iota(jnp.int32, (tq, 1), 0)
    w_bias = jnp.where((w_rel <= t_q) & (w_rel >= t_q - WINDOW), 0.0, NEG)
    cols = [slice(k * LANES, (k + 1) * LANES) for k in range(N_KV_HEADS)]
    blk_tile = lax.broadcasted_iota(jnp.int32, (HALF, 1), 0) >> _log2(tk // SEL_BLOCK)

    def add_alibi(s, kvh, rel):
        relf = rel.astype(F32)
        return jnp.concatenate([s[g * tq:(g + 1) * tq] + SLOPES[kvh * GQA + g] * relf for g in range(GQA)], axis=0)

    o_cmp, o_win = [], []
    tile_hits = [jnp.int32(0)] * n_tiles
    for kvh in range(N_KV_HEADS):
        qs = [q_ref[:, (kvh * GQA + g) * LANES:(kvh * GQA + g + 1) * LANES] for g in range(GQA)]
        qall = jnp.concatenate(qs, axis=0)
        pn = _masked_softmax(_dot_nt(qall, ckv_ref[0, :, cols[kvh]]), c_rel <= t_rel)
        o_cmp.append(_dot(pn.astype(BF16), ckv_ref[0, :, 2 * LANES + kvh * LANES:2 * LANES + (kvh + 1) * LANES]))
        psum = pn[0:tq] + pn[tq:2 * tq] + pn[2 * tq:3 * tq] + pn[3 * tq:4 * tq]
        hi, lw = _split_bf16(psum)
        imp = _dot(hi, ov_ref[...]) + _dot(lw, ov_ref[...])
        j = lane - HALF
        forced = (j == 0) | (j == cur) | (j == cur - 1)
        score = jnp.where(forced, BIG, jnp.where(j <= cur, imp, -BIG))
        sc_t = score.T[HALF:, :]
        cnt = [jnp.zeros((8, tq), F32) for _ in range(HALF // 8)]
        for jp in range(HALF):
            r = sc_t[jp:jp + 1, :]
            for v in range(HALF // 8):
                blk = sc_t[8 * v:8 * v + 8, :]
                if 8 * v > jp:
                    beats = _ones_where(r >= blk)
                elif 8 * v + 7 <= jp:
                    beats = _ones_where(r > blk)
                else:
                    beats = _ones_where(r > blk) + jnp.where(r == blk, _ones_where(sub > jp - 8 * v), 0.0)
                cnt[v] = cnt[v] + beats
        sel = jnp.concatenate(cnt, axis=0) < float(N_SEL)
        selbias = jnp.concatenate([jnp.zeros((HALF, tq), F32), jnp.where(sel, 0.0, NEG)], axis=0).T.astype(BF16)
        qsel_ref[kvh] = jnp.concatenate([jnp.where(lo, q, selbias) for q in qs], axis=0)
        any_sel = jnp.max(_ones_where(sel), axis=-1, keepdims=True)
        for w in range(-(-n_tiles // 8)):
            weight = jnp.zeros((HALF, 1), F32)
            for k in range(8):
                weight = jnp.where(blk_tile == 8 * w + k, float(8 ** k), weight)
            packed = jnp.sum(any_sel * weight).astype(jnp.int32)
            for k in range(min(8, n_tiles - 8 * w)):
                tile_hits[8 * w + k] = tile_hits[8 * w + k] + ((packed >> (3 * k)) & 7)
        s = _dot_nt(qall, kwp_ref[pl.ds(w_start, nwin), cols[kvh]])
        s = jnp.concatenate([s[g * tq:(g + 1) * tq] + w_bias for g in range(GQA)], axis=0)
        pw = jnp.exp(s - jnp.max(s, axis=-1, keepdims=True))
        o_win.append(_dot(pw.astype(BF16), vwd_ref[pl.ds(w_start, nwin), cols[kvh]])
                     * (1.0 / jnp.sum(pw, axis=-1, keepdims=True)))
        mrun_ref[kvh] = jnp.full((rows, LANES), NEG, F32)
        lrun_ref[kvh] = jnp.zeros((rows, LANES), F32)
        acc_ref[kvh] = jnp.zeros((rows, LANES), F32)

    def scores(kvh, jt, causal):
        k0 = pl.multiple_of(jt * tk, tk)
        k_rel = (k0 - t0) + lax.broadcasted_iota(jnp.int32, (1, tk), 1)
        s = add_alibi(_dot_nt(qsel_ref[kvh], ksa_ref[pl.ds(k0, tk), cols[kvh]]), kvh, k_rel)
        if causal:
            s = jnp.where(k_rel <= t_rel, s, NEG)
        s_ref[kvh, jt] = s
        mt = s[:, :LANES]
        for c in range(1, tk // LANES):
            mt = jnp.maximum(mt, s[:, c * LANES:(c + 1) * LANES])
        mrun_ref[kvh] = jnp.maximum(mrun_ref[kvh], mt)

    n_hit = jnp.int32(0)
    for jt in range(n_tiles):
        tile_ref[n_hit] = jt
        n_hit = n_hit + ((tile_hits[jt] > 0) & (jt < n_full)).astype(jnp.int32)

    def full_tile(i, carry):
        for kvh in range(N_KV_HEADS):
            scores(kvh, tile_ref[i], False)
        return carry

    lax.fori_loop(0, n_hit, full_tile, 0)
    for kvh in range(N_KV_HEADS):
        scores(kvh, n_full, True)
        mb_ref[kvh] = jnp.broadcast_to(jnp.max(mrun_ref[kvh], axis=-1, keepdims=True), (rows, tk))

    def weigh(kvh, jt):
        k0 = pl.multiple_of(jt * tk, tk)
        p = jnp.exp(s_ref[kvh, jt] - mb_ref[kvh])
        pt = p[:, :LANES]
        for c in range(1, tk // LANES):
            pt = pt + p[:, c * LANES:(c + 1) * LANES]
        lrun_ref[kvh] = lrun_ref[kvh] + pt
        acc_ref[kvh] = acc_ref[kvh] + _dot(p.astype(BF16), vsd_ref[pl.ds(k0, tk), cols[kvh]])

    def weigh_tile(i, carry):
        for kvh in range(N_KV_HEADS):
            weigh(kvh, tile_ref[i])
        return carry

    lax.fori_loop(0, n_hit, weigh_tile, 0)
    for kvh in range(N_KV_HEADS):
        weigh(kvh, n_full)
    head_out = []
    for kvh in range(N_KV_HEADS):
        o_slc = acc_ref[kvh] * (1.0 / jnp.sum(lrun_ref[kvh], axis=-1, keepdims=True))
        for g in range(GQA):
            h = kvh * GQA + g
            r = slice(g * tq, (g + 1) * tq)
            head_out.append(gates[:, h:h + 1] * o_cmp[kvh][r] + gates[:, N_HEADS + h:N_HEADS + h + 1] * o_slc[r]
                            + gates[:, 2 * N_HEADS + h:2 * N_HEADS + h + 1] * o_win[kvh][r])
    o = jnp.concatenate([jnp.where(lo, head_out[2 * i], head_out[2 * i + 1]) for i in range(N_HEADS // 2)], axis=1)
    o_ref[...] = _rms_norm(o, go_ref[...]).astype(BF16)


def _nsa_prompt(qpad, gate, ckv, ksa, vsd, kwp, vwd, ov, go, nb, seq, tq=128, tk=256):
    assert seq % tk == 0 and tk % tq == 0 and tq == LANES and seq >= WINDOW + tq and tk // SEL_BLOCK <= 7
    nwin = WINDOW + tq
    nq = seq // tq
    qrow = lambda w: pl.BlockSpec((tq, w), lambda b, i: (b * nq + i, 0))
    seqblk = lambda w: pl.BlockSpec((seq, w), lambda b, i: (b, 0))
    cs = lambda a: pl.BlockSpec(a.shape, lambda b, i: (0,) * a.ndim, pipeline_mode=pl.Buffered(1))
    rows = GQA * tq
    acc = pltpu.VMEM((N_KV_HEADS, rows, LANES), F32)
    return pl.pallas_call(
        functools.partial(_nsa_prompt_body, tq=tq, tk=tk, nwin=nwin),
        grid=(nb, nq),
        in_specs=[qrow(qpad.shape[1]), qrow(LANES), pl.BlockSpec((1,) + ckv.shape[1:], lambda b, i: (b, 0, 0)),
                  seqblk(2 * LANES), seqblk(2 * LANES), seqblk(2 * LANES), seqblk(2 * LANES), cs(ov), cs(go)],
        out_specs=qrow(ATTN_W),
        out_shape=jax.ShapeDtypeStruct((nb * seq, ATTN_W), BF16),
        scratch_shapes=[pltpu.VMEM((N_KV_HEADS, seq // tk, rows, tk), F32), pltpu.VMEM((N_KV_HEADS, rows, LANES), BF16),
                        acc, pltpu.VMEM((N_KV_HEADS, rows, tk), F32), acc, acc,
                        pltpu.SMEM((seq // tk,), jnp.int32)],
        compiler_params=_params(2), name="nsa_prompt",
    )(qpad, gate, ckv, ksa, vsd, kwp, vwd, ov, go)


T8 = 8


def _nsa_sample_body(pt_ref, cache_ref, q_ref, gate_ref, ckv_ref, win_ref, new_ref, e_ref, ov_ref, go_ref,
                     o_ref, buf_ref, sem_ref, *, n_pages, past, n_new, n_cmp):
    def make_copies(seq_idx, slot):
        return [pltpu.make_async_copy(cache_ref.at[pt_ref[seq_idx, j]],
                                      buf_ref.at[slot, :, :, pl.ds(j * PAGE_SIZE, PAGE_SIZE)], sem_ref.at[slot])
                for j in range(n_pages)]

    slot = _gather_step(make_copies)
    rows = N_KV_HEADS * GQA * T8
    ridx = lax.broadcasted_iota(jnp.int32, (rows, 1), 0)
    t_new = ridx & (n_new - 1)
    slope = _alibi_slopes(ridx >> _log2(T8))
    q = q_ref[0]
    new = new_ref[0]

    def tail(x):
        return jnp.concatenate([x, jnp.zeros((LANES - T8, x.shape[1]), F32)], axis=0).astype(BF16)

    def attend(kt, vt, knew, vnew, extra, n_past):
        rel = lax.broadcasted_iota(jnp.int32, (1, n_past + LANES), 1) - n_past
        s = jnp.concatenate([_dot(q, kt) if extra is None else _dot(*extra), _dot_nt(q, knew)], axis=1)
        s = s + slope * rel.astype(F32)
        p = _masked_softmax(s, (rel <= t_new) & (rel >= t_new - WINDOW) & (rel + past >= 0)
                            if extra is None else rel <= t_new)
        return _dot_nt(p[:, :n_past].astype(BF16), vt) + _dot(p[:, n_past:].astype(BF16), vnew)

    nh = ckv_ref.shape[1]
    n_i = lax.broadcasted_iota(jnp.int32, (1, nh), 1)
    dist = (past + t_new) - (n_i * CMP_STRIDE + (CMP_LEN - 1))
    ckv = ckv_ref[0]
    s = _dot_nt(q, ckv[:, :LANES]) - slope * dist.astype(F32)
    pn = _masked_softmax(s, (dist >= 0) & (n_i < n_cmp))
    o_cmp = _dot(pn.astype(BF16), ckv[:, LANES:])
    psum = jnp.concatenate(
        [sum(pn[(k * GQA + g) * T8:(k * GQA + g + 1) * T8] for g in range(GQA)) for k in range(N_KV_HEADS)], axis=0)
    hi, lw = _split_bf16(psum)
    imp = _dot(hi, ov_ref[...]) + _dot(lw, ov_ref[...])
    n_blk = past // SEL_BLOCK
    lane = lax.broadcasted_iota(jnp.int32, (N_KV_HEADS * T8, LANES), 1)
    forced = (lane == 0) | (lane == n_blk - 1)
    score = jnp.where(forced, BIG, jnp.where(lane < n_blk, imp, -BIG))
    cnt = jnp.zeros(score.shape, F32)
    for jp in range(n_blk):
        r = score[:, jp:jp + 1]
        cnt = cnt + _ones_where(r > score) + jnp.where(r == score, _ones_where(lane > jp), 0.0)
    bias = jnp.where(cnt < float(N_SEL - 1), 0.0, NEG).astype(BF16)
    selbias = jnp.concatenate([bias[k * T8:(k + 1) * T8] for k in range(N_KV_HEADS) for _ in range(GQA)], axis=0)
    kt = buf_ref[slot, 0].astype(BF16)
    vt = buf_ref[slot, 1].astype(BF16)
    q_aug = jnp.concatenate([q, selbias], axis=1)
    k_aug = jnp.concatenate([kt, e_ref[...]], axis=0)
    o_slc = attend(kt, vt, tail(new[:, 2 * KV_W:3 * KV_W]), tail(new[:, 3 * KV_W:4 * KV_W]), (q_aug, k_aug), past)
    nw = win_ref.shape[3]
    o_win = attend(win_ref[0, 0].astype(BF16), win_ref[0, 1].astype(BF16),
                   tail(new[:, 4 * KV_W:5 * KV_W]), tail(new[:, 5 * KV_W:6 * KV_W]), None, nw)
    gates = gate_ref[0]
    o = gates[:, 0:1] * o_cmp + gates[:, 1:2] * o_slc + gates[:, 2:3] * o_win
    o_r = pltpu.roll(o, HALF, axis=1)
    lo = lax.broadcasted_iota(jnp.int32, (T8, LANES), 1) < HALF
    groups = []
    for k in range(N_KV_HEADS):
        for i in range(GQA // 2):
            r_even = slice((k * GQA + 2 * i) * T8, (k * GQA + 2 * i + 1) * T8)
            r_odd = slice((k * GQA + 2 * i + 1) * T8, (k * GQA + 2 * i + 2) * T8)
            groups.append(jnp.where(lo, o[r_even], o_r[r_odd]) if k == 0 else jnp.where(lo, o_r[r_even], o[r_odd]))
    o_ref[0] = _rms_norm(jnp.concatenate(groups, axis=1), go_ref[...]).astype(BF16)


def _nsa_sample(page_table, cache_t, q_bd, gate_rows, ckv, win_t, new8, e_mat, ov, go, past, n_new):
    nseq, n_pages = page_table.shape
    assert past == n_pages * PAGE_SIZE and past % SEL_BLOCK == 0 and past // SEL_BLOCK <= LANES
    assert n_new <= T8 and T8 % n_new == 0 and past % CMP_STRIDE == 0
    per_seq = lambda a: pl.BlockSpec((1,) + a.shape[1:], lambda i, pt: (i,) + (0,) * (a.ndim - 1))
    cs = lambda a: pl.BlockSpec(a.shape, lambda i, pt: (0,) * a.ndim, pipeline_mode=pl.Buffered(1))
    return pl.pallas_call(
        functools.partial(_nsa_sample_body, n_pages=n_pages, past=past, n_new=n_new,
                          n_cmp=past // CMP_STRIDE - 1),
        grid_spec=pltpu.PrefetchScalarGridSpec(
            num_scalar_prefetch=1, grid=(nseq,),
            in_specs=[pl.BlockSpec(memory_space=pl.ANY), per_seq(q_bd), per_seq(gate_rows), per_seq(ckv),
                      per_seq(win_t), per_seq(new8), cs(e_mat), cs(ov), cs(go)],
            out_specs=pl.BlockSpec((1, T8, ATTN_W), lambda i, pt: (i, 0, 0)),
            scratch_shapes=[pltpu.VMEM((2, 2, KV_W, past), F32), pltpu.SemaphoreType.DMA((2,))]),
        out_shape=jax.ShapeDtypeStruct((nseq, T8, ATTN_W), BF16),
        compiler_params=_params(1), name="nsa_sample",
    )(page_table, cache_t, q_bd, gate_rows, ckv, win_t, new8, e_mat, ov, go)


def _overlap(n_rows, n_cmp, lane0):
    n = np.arange(n_rows)[:, None]
    j = np.arange(LANES)[None, :] - lane0
    c_end = n * CMP_STRIDE + (CMP_LEN - 1)
    ov = (j * SEL_BLOCK <= c_end) & (j * SEL_BLOCK + SEL_BLOCK > c_end + 1 - CMP_LEN) & (n < n_cmp) & (j >= 0)
    return jnp.asarray(ov, BF16)


def _compress_weights(k_pos, k_w1, k_b1, k_w2, v_pos, v_w1, v_b1, v_w2):
    eye = jnp.eye(4, dtype=F32)
    feat = 4 * HEAD_DIM
    w1 = jnp.stack([k_w1, k_w1, v_w1, v_w1])
    half = lambda a: jnp.einsum("cldh,ce->lcdeh", a, eye).reshape(CMP_STRIDE, feat, 4 * CMP_HIDDEN)
    wab = jnp.concatenate([half(w1[:, :CMP_STRIDE]), half(w1[:, CMP_STRIDE:])], axis=2).astype(BF16)
    pos = jnp.stack([k_pos, k_pos, v_pos, v_pos])
    flat = lambda a: jnp.broadcast_to(jnp.transpose(a, (1, 0, 2)).reshape(CMP_STRIDE, 1, feat), (CMP_STRIDE, 8, feat))
    pos16 = jnp.concatenate([flat(pos[:, :CMP_STRIDE]), flat(pos[:, CMP_STRIDE:])], axis=1).astype(BF16)
    b1 = jnp.concatenate([k_b1, k_b1, v_b1, v_b1]).reshape(1, -1)
    z = jnp.zeros((CMP_HIDDEN, HEAD_DIM), F32)
    w2p = jnp.concatenate([
        jnp.concatenate([k_w2, z, z, z, z, z, z, z], axis=1), jnp.concatenate([z, z, k_w2, z, z, z, z, z], axis=1),
        jnp.concatenate([z, z, z, z, v_w2, v_w2, z, z], axis=1), jnp.concatenate([z, z, z, z, z, z, v_w2, v_w2], axis=1),
    ], axis=0).astype(BF16)
    w2s = jnp.concatenate([
        jnp.concatenate([k_w2, z, z, z], axis=1), jnp.concatenate([z, k_w2, z, z], axis=1),
        jnp.concatenate([z, z, v_w2, z], axis=1), jnp.concatenate([z, z, z, v_w2], axis=1)], axis=0).astype(BF16)
    return wab, pos16, b1, w2p, w2s


def _proj_weight(w_in):
    d = w_in.shape[0]
    g0 = ATTN_W + 6 * KV_W
    wg = jnp.concatenate([w_in[:, g0:g0 + 3 * N_HEADS], jnp.zeros((d, LANES - 3 * N_HEADS), F32)], axis=1)
    return jnp.concatenate([w_in[:, :ATTN_W] * HEAD_DIM ** -0.5, w_in[:, ATTN_W:g0], wg,
                            w_in[:, g0 + 3 * N_HEADS:]], axis=1).astype(BF16)


def _feature_major(a):
    lead = a.shape[:-4]
    n = len(lead)
    a = jnp.transpose(a, tuple(range(n)) + (n + 1, n + 2, n + 3, n))
    return a.reshape(lead + (2, KV_W, a.shape[-1]))


def _token_major(a, nb):
    return jnp.transpose(a.reshape(nb, 2, N_KV_HEADS, HEAD_DIM, a.shape[-1]), (0, 4, 1, 2, 3))


def kernel(x_prompt, x_sample, cache_cmp_kv, cache_slc_kv, state_win_kv, page_table, ln1_g, ln1_b, ffn1_w_in, ffn1_w_out, w_in, cmp_k_pos, cmp_k_w1, cmp_k_b1, cmp_k_w2, cmp_v_pos, cmp_v_w1, cmp_v_b1, cmp_v_w2, sgu_ln_g, sgu_ln_b, sgu_w, sgu_b, out_norm_g, w_out, ln2_g, ln2_b, ffn2_w_in, ffn2_w_out, ln3_g, ln3_b):
    depth = w_in.shape[0]
    nb, seq, d = x_prompt.shape
    ns, n_new, _ = x_sample.shape
    n_pages = page_table.shape[1]
    past = n_pages * PAGE_SIZE
    d_ff = ffn1_w_out.shape[1]
    alpha = (2.0 * depth) ** 0.25
    row = lambda a: a.reshape(1, -1)
    hp = x_prompt.reshape(nb * seq, d)
    hs = x_sample.reshape(ns * n_new, d)
    kv5 = (N_KV_HEADS, HEAD_DIM)
    outs = [[] for _ in range(7)]
    ov_p = _overlap(seq // CMP_STRIDE, seq // CMP_STRIDE - 1, HALF)
    ov_s = _overlap(past // CMP_STRIDE, past // CMP_STRIDE - 1, 0)
    e_mat = jnp.asarray(np.arange(LANES)[:, None] == np.arange(past)[None, :] // SEL_BLOCK, BF16)
    for l in range(depth):
        f1 = (ffn1_w_in[l, :, :d_ff].astype(BF16), ffn1_w_in[l, :, d_ff:].astype(BF16), ffn1_w_out[l].astype(BF16),
              row(ln1_g[l]), row(ln1_b[l]))
        f2 = (ffn2_w_in[l, :, :d_ff].astype(BF16), ffn2_w_in[l, :, d_ff:].astype(BF16), ffn2_w_out[l].astype(BF16),
              row(ln3_g[l]), row(ln3_b[l]))
        wp = _proj_weight(w_in[l])
        wab, pos16, b1, w2p, w2s = _compress_weights(cmp_k_pos[l], cmp_k_w1[l], cmp_k_b1[l], cmp_k_w2[l],
                                                     cmp_v_pos[l], cmp_v_w1[l], cmp_v_b1[l], cmp_v_w2[l])
        sg, sb = row(sgu_ln_g[l]), row(sgu_ln_b[l])
        go_a, go_g = row(out_norm_g[l, :ATTN_W]), row(out_norm_g[l, ATTN_W:])
        wm = w_out[l].astype(BF16)
        mix_p = jnp.tril(sgu_w[l]).astype(BF16)
        mb_p = jnp.repeat(sgu_b[l].T, GMLP_DIM, axis=1)
        r = min(n_new, CHUNK)
        mix_s = jax.vmap(lambda w: jnp.kron(jnp.eye(CHUNK // r, dtype=F32), w))(jnp.tril(sgu_w[l, :, :r, :r])).astype(BF16)
        mb_s = jnp.tile(jnp.repeat(sgu_b[l, :, :r].T, GMLP_DIM, axis=1), (CHUNK // r, 1))

        hp = _ffn_ln(hp, *f1, alpha)
        (qpad, cmpk_p, cmpv_p, _, _, cmp_t, slc_t, win_t, ksa, vsd, kwp, vwd, gate, ogn, _) = _proj(
            hp, wp, sg, sb, mix_p, mb_p, go_g, seq)
        ckv = _compress_prompt(cmpk_p, cmpv_p, nb, wab, pos16, b1, w2p)
        oan = _nsa_prompt(qpad, gate, ckv, ksa, vsd, kwp, vwd, ov_p, go_a, nb, seq)
        hp = _ffn_ln(hp, *f2, alpha, merge_args=(oan, ogn, wm, row(ln2_g[l]), row(ln2_b[l])))
        outs[0].append(_token_major(cmp_t, nb))
        outs[2].append(_token_major(slc_t, nb))
        outs[4].append(_token_major(win_t[:, :, seq - min(WINDOW, seq):], nb))

        hs = _ffn_ln(hs, *f1, alpha)
        (qpad, cmpk_s, cmpv_s, slc_s, win_s, _, _, _, _, _, _, _, gate, ogn, v_s) = _proj(
            hs, wp, sg, sb, mix_s, mb_s, go_g, ns * n_new)
        cmp_s = jnp.concatenate([cmpk_s, cmpv_s], axis=1)
        ckv = _compress_sample(page_table, _feature_major(cache_cmp_kv[l]), wab, pos16, b1, w2s)
        q5 = qpad.reshape(ns, n_new, N_KV_HEADS, GQA, LANES)[..., :HEAD_DIM]
        q5 = jnp.pad(jnp.transpose(q5, (0, 2, 3, 1, 4)), ((0, 0), (0, 0), (0, 0), (0, T8 - n_new), (0, 0)))
        zq = jnp.zeros_like(q5[:, 0])
        q_bd = jnp.stack([jnp.concatenate([q5[:, 0], zq], axis=-1), jnp.concatenate([zq, q5[:, 1]], axis=-1)],
                         axis=1).reshape(ns, N_KV_HEADS * GQA * T8, LANES)
        g5 = gate[:, :3 * N_HEADS].reshape(ns, n_new, 3, N_KV_HEADS, GQA)
        g5 = jnp.pad(jnp.transpose(g5, (0, 3, 4, 1, 2)), ((0, 0), (0, 0), (0, 0), (0, T8 - n_new), (0, 0)))
        gate_rows = g5.reshape(ns, N_KV_HEADS * GQA * T8, 3)
        new8 = jnp.pad(jnp.concatenate([cmp_s, slc_s, win_s], axis=1).reshape(ns, n_new, 6 * KV_W),
                       ((0, 0), (0, T8 - n_new), (0, 0)))
        oan = _nsa_sample(page_table, _feature_major(cache_slc_kv[l]), q_bd, gate_rows, ckv,
                          _feature_major(state_win_kv[l]), new8, e_mat, ov_s, go_a, past, n_new)
        oan = oan[:, :n_new].reshape(ns * n_new, ATTN_W)
        hs = _ffn_ln(hs, *f2, alpha, merge_args=(oan, ogn, wm, row(ln2_g[l]), row(ln2_b[l])))
        win_all = jnp.concatenate([state_win_kv[l], win_s.reshape(ns, n_new, 2, *kv5)], axis=1)
        outs[1].append(cmp_s.reshape(ns, n_new, 2, *kv5))
        outs[3].append(slc_s.reshape(ns, n_new, 2, *kv5))
        outs[5].append(win_all[:, -min(WINDOW, past + n_new):])
        outs[6].append(v_s.reshape(ns, n_new, N_GMLP, GMLP_DIM))
    st = [jnp.stack(o, axis=0) for o in outs]
    return (hp.reshape(nb, seq, d), hs.reshape(ns, n_new, d), st[0], st[1], st[2], st[3], st[4], st[5], st[6])
```

```python
import functools

import numpy as np
import jax
import jax.numpy as jnp
from jax import lax
from jax.experimental import pallas as pl
from jax.experimental.pallas import tpu as pltpu

HEAD_DIM = 64
N_HEADS = 8
N_KV_HEADS = 2
GQA = N_HEADS // N_KV_HEADS
ATTN_W = N_HEADS * HEAD_DIM
KV_W = N_KV_HEADS * HEAD_DIM
CMP_LEN = 32
CMP_STRIDE = 16
CMP_HIDDEN = 64
SEL_BLOCK = 64
N_SEL = 16
WINDOW = 512
N_GMLP = 8
GMLP_DIM = 64
GMLP_W = N_GMLP * GMLP_DIM
CHUNK = 128
PAGE_SIZE = 128
LN_EPS = 1e-5
NEG = -1e30
BIG = 1e9

LANES = 128
HALF = LANES // 2
VMEM_LIMIT = 56 * 1024 * 1024
SLOPES = [2.0 ** -(h + 1) for h in range(N_HEADS)]

F32 = jnp.float32
BF16 = jnp.bfloat16

_Q0, _KV0, _G0, _U0, _V0, _PW = 0, ATTN_W, ATTN_W + 6 * KV_W, ATTN_W + 6 * KV_W + LANES, \
    ATTN_W + 6 * KV_W + LANES + GMLP_W, ATTN_W + 6 * KV_W + LANES + 2 * GMLP_W


def _dot(a, b):
    return jnp.dot(a, b, preferred_element_type=F32)


def _dot_nt(a, b):
    return lax.dot_general(a, b, (((1,), (1,)), ((), ())), preferred_element_type=F32)


def _layer_norm(y, g, b):
    mu = jnp.mean(y, axis=-1, keepdims=True)
    d = y - mu
    var = jnp.mean(d * d, axis=-1, keepdims=True)
    return d * lax.rsqrt(var + LN_EPS) * g + b


def _rms_norm(y, g):
    return y * lax.rsqrt(jnp.mean(y * y, axis=-1, keepdims=True) + LN_EPS) * g


def _ones_where(mask):
    return jnp.where(mask, 1.0, 0.0)


def _alibi_lanes(lane, hi, lo):
    return jnp.where(lane == HALF, hi, jnp.where(lane == HALF + 1, lo, 0.0))


def _log2(n):
    assert n > 0 and n & (n - 1) == 0, n
    return n.bit_length() - 1


def _const_spec(shape):
    nd = len(shape)
    return pl.BlockSpec(shape, lambda *_: (0,) * nd, pipeline_mode=pl.Buffered(1))


def _params(n_grid):
    return pltpu.CompilerParams(dimension_semantics=("arbitrary",) * n_grid, vmem_limit_bytes=VMEM_LIMIT)


def _ffn_body(*refs, alpha, merge):
    if merge:
        (h_ref, oa_ref, og_ref, wm_ref, gm_ref, bm_ref,
         wa_ref, wb_ref, wo_ref, g_ref, b_ref, o_ref) = refs
        cat = jnp.concatenate([oa_ref[...], og_ref[...]], axis=1)
        x = _layer_norm(alpha * h_ref[...] + _dot(cat, wm_ref[...]), gm_ref[...], bm_ref[...])
    else:
        x_ref, wa_ref, wb_ref, wo_ref, g_ref, b_ref, o_ref = refs
        x = x_ref[...]
    xb = x.astype(BF16)
    a = _dot(xb, wa_ref[...])
    b = _dot(xb, wb_ref[...])
    hid = ((a * jax.nn.sigmoid(a)) * b).astype(BF16)
    y = alpha * x + 0.5 * _dot(hid, wo_ref[...])
    o_ref[...] = _layer_norm(y, g_ref[...], b_ref[...])


def _ffn_ln(x, wa, wb, wo, g, b, alpha, merge_args=None, tm=512):
    t, d = x.shape
    tm = min(tm, t)
    assert t % tm == 0
    row = lambda w: pl.BlockSpec((tm, w), lambda i: (i, 0))
    ins, specs = [x], [row(d)]
    if merge_args is not None:
        oa, og, wm, gm, bm = merge_args
        ins += [oa, og, wm, gm, bm]
        specs += [row(oa.shape[1]), row(og.shape[1]), _const_spec(wm.shape), _const_spec(gm.shape),
                  _const_spec(bm.shape)]
    ins += [wa, wb, wo, g, b]
    specs += [_const_spec(a.shape) for a in (wa, wb, wo, g, b)]
    return pl.pallas_call(
        functools.partial(_ffn_body, alpha=alpha, merge=merge_args is not None),
        grid=(t // tm,), in_specs=specs, out_specs=row(d),
        out_shape=jax.ShapeDtypeStruct((t, d), F32), compiler_params=_params(1),
        name="ffn_merge_ln" if merge_args is not None else "ffn_ln",
    )(*ins)


def _proj_body(h_ref, w_ref, sg_ref, sb_ref, mix_ref, mb_ref, go_ref,
               qpad_ref, cmpk_ref, cmpv_ref, slc_ref, win_ref, cmpt_ref, slct_ref, wint_ref,
               ksa_ref, vsd_ref, kwp_ref, vwd_ref, gate_ref, ogn_ref, v_ref, *, tm, seq):
    p = _dot(h_ref[...].astype(BF16), w_ref[...])
    lane = lax.broadcasted_iota(jnp.int32, (tm, LANES), 1)
    lo = lane < HALF
    tok = lax.rem(pl.program_id(0) * tm, seq) + lax.broadcasted_iota(jnp.int32, (tm, LANES), 0)
    for i in range(ATTN_W // LANES):
        x = p[:, _Q0 + i * LANES:_Q0 + (i + 1) * LANES]
        xr = pltpu.roll(x, HALF, axis=1)
        qpad_ref[:, (2 * i) * LANES:(2 * i + 1) * LANES] = jnp.where(
            lo, x, _alibi_lanes(lane, LANES * SLOPES[2 * i], SLOPES[2 * i])).astype(BF16)
        qpad_ref[:, (2 * i + 1) * LANES:(2 * i + 2) * LANES] = jnp.where(
            lo, xr, _alibi_lanes(lane, LANES * SLOPES[2 * i + 1], SLOPES[2 * i + 1])).astype(BF16)
    for s, feat_ref in enumerate((cmpt_ref, slct_ref, wint_ref)):
        feat_ref[0] = p[:, _KV0 + 2 * s * KV_W:_KV0 + 2 * (s + 1) * KV_W].T
    cmpk_ref[...] = p[:, _KV0:_KV0 + KV_W]
    cmpv_ref[...] = p[:, _KV0 + KV_W:_KV0 + 2 * KV_W]
    slc_ref[...] = p[:, _KV0 + 2 * KV_W:_KV0 + 4 * KV_W]
    win_ref[...] = p[:, _KV0 + 4 * KV_W:_KV0 + 6 * KV_W]
    onehot = _ones_where(lane - HALF == (tok >> _log2(SEL_BLOCK)))
    ks = p[:, _KV0 + 2 * KV_W:_KV0 + 3 * KV_W]
    ksr = pltpu.roll(ks, HALF, axis=1)
    ksa_ref[:, :LANES] = jnp.where(lo, ks, onehot).astype(BF16)
    ksa_ref[:, LANES:] = jnp.where(lo, ksr, onehot).astype(BF16)
    vs = p[:, _KV0 + 3 * KV_W:_KV0 + 4 * KV_W]
    vsr = pltpu.roll(vs, HALF, axis=1)
    vsd_ref[:, :LANES] = jnp.where(lo, vs, vsr).astype(BF16)
    vsd_ref[:, LANES:] = jnp.where(lo, vsr, vs).astype(BF16)
    kw = p[:, _KV0 + 4 * KV_W:_KV0 + 5 * KV_W]
    kwr = pltpu.roll(kw, HALF, axis=1)
    pos_pair = _alibi_lanes(lane, (tok >> _log2(LANES)).astype(F32), (tok & (LANES - 1)).astype(F32))
    kwp_ref[:, :LANES] = jnp.where(lo, kw, pos_pair).astype(BF16)
    kwp_ref[:, LANES:] = jnp.where(lo, kwr, pos_pair).astype(BF16)
    vw = p[:, _KV0 + 5 * KV_W:_KV0 + 6 * KV_W]
    vwr = pltpu.roll(vw, HALF, axis=1)
    vwd_ref[:, :LANES] = jnp.where(lo, vw, vwr).astype(BF16)
    vwd_ref[:, LANES:] = jnp.where(lo, vwr, vw).astype(BF16)
    gate_ref[...] = jax.nn.sigmoid(p[:, _G0:_G0 + LANES])
    u = jax.nn.gelu(p[:, _U0:_U0 + GMLP_W])
    v = _layer_norm(jax.nn.gelu(p[:, _V0:_V0 + GMLP_W]), sg_ref[...], sb_ref[...])
    v_ref[...] = v
    vb = v.astype(BF16)
    lo_c = lax.broadcasted_iota(jnp.int32, (CHUNK, LANES), 1) < HALF
    for c in range(tm // CHUNK):
        rows = slice(c * CHUNK, (c + 1) * CHUNK)
        zs = []
        for i in range(GMLP_W // LANES):
            vp = vb[rows, i * LANES:(i + 1) * LANES]
            zs.append(jnp.where(lo_c, _dot(mix_ref[2 * i], vp), _dot(mix_ref[2 * i + 1], vp)))
        z = jnp.concatenate(zs, axis=1) + mb_ref[...]
        ogn_ref[rows, :] = _rms_norm(u[rows, :] * z, go_ref[...]).astype(BF16)


def _proj(h, w, sg, sb, mix, mb, go, seq, tm=512):
    t, d = h.shape
    tm = min(tm, t)
    assert t % tm == 0 and tm % CHUNK == 0 and seq % tm == 0 and t % seq == 0 and seq // SEL_BLOCK <= HALF
    nq = seq // tm
    row = lambda width: pl.BlockSpec((tm, width), lambda i: (i, 0))
    feat = pl.BlockSpec((1, 2 * KV_W, tm), lambda i: (i // nq, 0, i % nq))
    tok_major = [(2 * ATTN_W, BF16), (KV_W, F32), (KV_W, F32), (2 * KV_W, F32), (2 * KV_W, F32)]
    rest = [(2 * LANES, BF16), (2 * LANES, BF16), (2 * LANES, BF16), (2 * LANES, BF16), (LANES, F32),
            (GMLP_W, BF16), (GMLP_W, F32)]
    sds = jax.ShapeDtypeStruct
    return pl.pallas_call(
        functools.partial(_proj_body, tm=tm, seq=seq),
        grid=(t // tm,),
        in_specs=[row(d)] + [_const_spec(a.shape) for a in (w, sg, sb, mix, mb, go)],
        out_specs=[row(wd) for wd, _ in tok_major] + [feat] * 3 + [row(wd) for wd, _ in rest],
        out_shape=[sds((t, wd), dt) for wd, dt in tok_major] + [sds((t // seq, 2 * KV_W, seq), F32)] * 3
        + [sds((t, wd), dt) for wd, dt in rest],
        compiler_params=_params(1), name="proj",
    )(h, w, sg, sb, mix, mb, go)


def _compress_math(rows_at, nh, wab_ref, pos_ref, b1_ref, w2_ref):
    hw = b1_ref.shape[1]
    ab = None
    for l in range(CMP_STRIDE):
        lhs = jnp.concatenate([rows_at(l).astype(BF16), pos_ref[l]], axis=0)
        d = _dot(lhs, wab_ref[l])
        ab = d if ab is None else ab + d
    c = ab[nh:nh + 1, :hw] + ab[nh + 8:nh + 9, hw:] + b1_ref[...]
    hid = ab[:nh, :hw] + pltpu.roll(ab[:nh, hw:], nh - 1, axis=0) + c
    return _dot(jax.nn.gelu(hid).astype(BF16), w2_ref[...])


def _strided_rows(k_ref, v_ref, nh):
    return lambda l: jnp.concatenate([k_ref[pl.ds(l, nh, stride=CMP_STRIDE), :],
                                      v_ref[pl.ds(l, nh, stride=CMP_STRIDE), :]], axis=1)


def _compress_body(xk_ref, xv_ref, wab_ref, pos_ref, b1_ref, w2_ref, o_ref):
    nh = o_ref.shape[1]
    out = _compress_math(_strided_rows(xk_ref, xv_ref, nh), nh, wab_ref, pos_ref, b1_ref, w2_ref)
    lane = lax.broadcasted_iota(jnp.int32, (nh, LANES), 1)
    c_end = lax.broadcasted_iota(jnp.int32, (nh, LANES), 0) * CMP_STRIDE + (CMP_LEN - 1)
    pair = _alibi_lanes(lane, (c_end >> _log2(LANES)).astype(F32), (c_end & (LANES - 1)).astype(F32))
    for k in range(N_KV_HEADS):
        o_ref[0, :, k * LANES:(k + 1) * LANES] = (out[:, k * LANES:(k + 1) * LANES] + pair).astype(o_ref.dtype)
    o_ref[0, :, N_KV_HEADS * LANES:] = out[:, N_KV_HEADS * LANES:].astype(o_ref.dtype)


def _compress_prompt(xk, xv, nb, wab, pos, b1, w2):
    seq = xk.shape[0] // nb
    nh = seq // CMP_STRIDE
    return pl.pallas_call(
        _compress_body, grid=(nb,),
        in_specs=[pl.BlockSpec((seq, KV_W), lambda i: (i, 0))] * 2 + [_const_spec(a.shape) for a in (wab, pos, b1, w2)],
        out_specs=pl.BlockSpec((1, nh, w2.shape[1]), lambda i: (i, 0, 0)),
        out_shape=jax.ShapeDtypeStruct((nb, nh, w2.shape[1]), BF16),
        compiler_params=_params(1), name="compress_prompt",
    )(xk, xv, wab, pos, b1, w2)


def _gather_step(make_copies):
    i, n = pl.program_id(0), pl.num_programs(0)
    slot = lax.rem(i, 2)

    @pl.when(i == 0)
    def _():
        for cp in make_copies(i, slot):
            cp.start()

    @pl.when(i + 1 < n)
    def _():
        for cp in make_copies(i + 1, 1 - slot):
            cp.start()

    for cp in make_copies(i, slot):
        cp.wait()
    return slot


def _compress_sample_body(pt_ref, cache_ref, perm_ref, wab_ref, pos_ref, b1_ref, w2_ref, o_ref, buf_ref, xs_ref,
                          sem_ref, *, n_pages):
    def make_copies(seq_idx, slot):
        return [pltpu.make_async_copy(cache_ref.at[pt_ref[seq_idx, j]], buf_ref.at[slot, j], sem_ref.at[slot])
                for j in range(n_pages)]

    slot = _gather_step(make_copies)
    per_page = PAGE_SIZE // CMP_STRIDE

    feat, page = buf_ref.shape[3], buf_ref.shape[4]
    xt = buf_ref[slot].reshape(n_pages * 2 * feat, page)
    z = _dot_nt(perm_ref[...], xt.astype(BF16))
    for j in range(n_pages):
        for c in range(2):
            col = slice((2 * j + c) * feat, (2 * j + c + 1) * feat)
            for l in range(CMP_STRIDE):
                xs_ref[c, l, j * per_page:(j + 1) * per_page, :] = z[l * per_page:(l + 1) * per_page, col]
    nh = o_ref.shape[1]
    rows_at = lambda l: jnp.concatenate([xs_ref[0, l], xs_ref[1, l]], axis=1)
    o_ref[0] = _compress_math(rows_at, nh, wab_ref, pos_ref, b1_ref, w2_ref).astype(o_ref.dtype)


def _compress_sample(page_table, cache_t, wab, pos, b1, w2):
    nseq, n_pages = page_table.shape
    _, _, feat, page = cache_t.shape
    nh = n_pages * page // CMP_STRIDE
    cs = lambda a: pl.BlockSpec(a.shape, lambda i, pt: (0,) * a.ndim, pipeline_mode=pl.Buffered(1))
    tok = np.arange(page)
    perm = jnp.asarray((tok % CMP_STRIDE * (page // CMP_STRIDE) + tok // CMP_STRIDE)[None, :] == tok[:, None], BF16)
    return pl.pallas_call(
        functools.partial(_compress_sample_body, n_pages=n_pages),
        grid_spec=pltpu.PrefetchScalarGridSpec(
            num_scalar_prefetch=1, grid=(nseq,),
            in_specs=[pl.BlockSpec(memory_space=pl.ANY)] + [cs(a) for a in (perm, wab, pos, b1, w2)],
            out_specs=pl.BlockSpec((1, nh, w2.shape[1]), lambda i, pt: (i, 0, 0)),
            scratch_shapes=[pltpu.VMEM((2, n_pages, 2, feat, page), F32), pltpu.VMEM((2, CMP_STRIDE, nh, feat), F32),
                            pltpu.SemaphoreType.DMA((2,))]),
        out_shape=jax.ShapeDtypeStruct((nseq, nh, w2.shape[1]), BF16),
        compiler_params=_params(1), name="compress_sample",
    )(page_table, cache_t, perm, wab, pos, b1, w2)


def _masked_softmax(s, mask):
    sm = jnp.where(mask, s, NEG)
    m = jnp.max(sm, axis=-1, keepdims=True)
    p = jnp.where(mask, jnp.exp(sm - m), 0.0)
    l = jnp.sum(p, axis=-1, keepdims=True)
    return p * (1.0 / jnp.maximum(l, 1e-30))


def _split_bf16(x):
    hi = x.astype(BF16)
    return hi, (x - hi.astype(F32)).astype(BF16)


def _alibi_slopes(head_idx):
    out = jnp.zeros(head_idx.shape, F32)
    for h in range(N_HEADS):
        out = jnp.where(head_idx == h, SLOPES[h], out)
    return out


def _nsa_prompt_body(q_ref, gate_ref, ckv_ref, ksa_ref, vsd_ref, kwp_ref, vwd_ref, ov_ref, go_ref,
                     o_ref, s_ref, qsel_ref, mrun_ref, mb_ref, lrun_ref, acc_ref, tile_ref, *, tq, tk, nwin):
    n_tiles = s_ref.shape[1]
    t0 = pl.program_id(1) * tq
    rows = GQA * tq
    lane = lax.broadcasted_iota(jnp.int32, (tq, LANES), 1)
    lo = lane < HALF
    t_rel = lax.broadcasted_iota(jnp.int32, (rows, 1), 0) & (tq - 1)
    cur = (t0 + lax.broadcasted_iota(jnp.int32, (tq, 1), 0)) >> _log2(SEL_BLOCK)
    sub = lax.broadcasted_iota(jnp.int32, (8, tq), 0)
    gates = gate_ref[...]
    nh = ckv_ref.shape[1]
    c_rel = lax.broadcasted_iota(jnp.int32, (1, nh), 1) * CMP_STRIDE + (CMP_LEN - 1) - t0
    n_full = t0 // tk
    w_start = pl.multiple_of(jnp.maximum(t0 - WINDOW, 0), tq)
    w_rel = (w_start - t0) + lax.broadcasted_iota(jnp.int32, (1, nwin), 1)
    t_q = lax.broadcasted_iota(jnp.int32, (tq, 1), 0)
    w_bias = jnp.where((w_rel <= t_q) & (w_rel >= t_q - WINDOW), 0.0, NEG)
    cols = [slice(k * LANES, (k + 1) * LANES) for k in range(N_KV_HEADS)]
    blk_tile = lax.broadcasted_iota(jnp.int32, (HALF, 1), 0) >> _log2(tk // SEL_BLOCK)

    def add_alibi(s, kvh, rel):
        relf = rel.astype(F32)
        return jnp.concatenate([s[g * tq:(g + 1) * tq] + SLOPES[kvh * GQA + g] * relf for g in range(GQA)], axis=0)

    o_cmp, o_win = [], []
    tile_hits = [jnp.int32(0)] * n_tiles
    for kvh in range(N_KV_HEADS):
        qs = [q_ref[:, (kvh * GQA + g) * LANES:(kvh * GQA + g + 1) * LANES] for g in range(GQA)]
        qall = jnp.concatenate(qs, axis=0)
        pn = _masked_softmax(_dot_nt(qall, ckv_ref[0, :, cols[kvh]]), c_rel <= t_rel)
        o_cmp.append(_dot(pn.astype(BF16), ckv_ref[0, :, 2 * LANES + kvh * LANES:2 * LANES + (kvh + 1) * LANES]))
        psum = pn[0:tq] + pn[tq:2 * tq] + pn[2 * tq:3 * tq] + pn[3 * tq:4 * tq]
        hi, lw = _split_bf16(psum)
        imp = _dot(hi, ov_ref[...]) + _dot(lw, ov_ref[...])
        j = lane - HALF
        forced = (j == 0) | (j == cur) | (j == cur - 1)
        score = jnp.where(forced, BIG, jnp.where(j <= cur, imp, -BIG))
        sc_t = score.T[HALF:, :]
        cnt = [jnp.zeros((8, tq), F32) for _ in range(HALF // 8)]
        for jp in range(HALF):
            r = sc_t[jp:jp + 1, :]
            for v in range(HALF // 8):
                blk = sc_t[8 * v:8 * v + 8, :]
                if 8 * v > jp:
                    beats = _ones_where(r >= blk)
                elif 8 * v + 7 <= jp:
                    beats = _ones_where(r > blk)
                else:
                    beats = _ones_where(r > blk) + jnp.where(r == blk, _ones_where(sub > jp - 8 * v), 0.0)
                cnt[v] = cnt[v] + beats
        sel = jnp.concatenate(cnt, axis=0) < float(N_SEL)
        selbias = jnp.concatenate([jnp.zeros((HALF, tq), F32), jnp.where(sel, 0.0, NEG)], axis=0).T.astype(BF16)
        qsel_ref[kvh] = jnp.concatenate([jnp.where(lo, q, selbias) for q in qs], axis=0)
        any_sel = jnp.max(_ones_where(sel), axis=-1, keepdims=True)
        for w in range(-(-n_tiles // 8)):
            weight = jnp.zeros((HALF, 1), F32)
            for k in range(8):
                weight = jnp.where(blk_tile == 8 * w + k, float(8 ** k), weight)
            packed = jnp.sum(any_sel * weight).astype(jnp.int32)
            for k in range(min(8, n_tiles - 8 * w)):
                tile_hits[8 * w + k] = tile_hits[8 * w + k] + ((packed >> (3 * k)) & 7)
        s = _dot_nt(qall, kwp_ref[pl.ds(w_start, nwin), cols[kvh]])
        s = jnp.concatenate([s[g * tq:(g + 1) * tq] + w_bias for g in range(GQA)], axis=0)
        pw = jnp.exp(s - jnp.max(s, axis=-1, keepdims=True))
        o_win.append((_dot(pw.astype(BF16), vwd_ref[pl.ds(w_start, nwin), cols[kvh]]),
                      1.0 / jnp.sum(pw, axis=-1, keepdims=True)))
        mrun_ref[kvh] = jnp.full((rows, LANES), NEG, F32)
        lrun_ref[kvh] = jnp.zeros((rows, LANES), F32)
        acc_ref[kvh] = jnp.zeros((rows, LANES), F32)

    def scores(kvh, jt, causal):
        k0 = pl.multiple_of(jt * tk, tk)
        k_rel = (k0 - t0) + lax.broadcasted_iota(jnp.int32, (1, tk), 1)
        s = add_alibi(_dot_nt(qsel_ref[kvh], ksa_ref[pl.ds(k0, tk), cols[kvh]]), kvh, k_rel)
        if causal:
            s = jnp.where(k_rel <= t_rel, s, NEG)
        s_ref[kvh, jt] = s
        mt = s[:, :LANES]
        for c in range(1, tk // LANES):
            mt = jnp.maximum(mt, s[:, c * LANES:(c + 1) * LANES])
        mrun_ref[kvh] = jnp.maximum(mrun_ref[kvh], mt)

    n_hit = jnp.int32(0)
    for jt in range(n_tiles):
        tile_ref[n_hit] = jt
        n_hit = n_hit + ((tile_hits[jt] > 0) & (jt < n_full)).astype(jnp.int32)

    def full_tile(i, carry):
        for kvh in range(N_KV_HEADS):
            scores(kvh, tile_ref[i], False)
        return carry

    lax.fori_loop(0, n_hit, full_tile, 0)
    for kvh in range(N_KV_HEADS):
        scores(kvh, n_full, True)
        mb_ref[kvh] = jnp.broadcast_to(jnp.max(mrun_ref[kvh], axis=-1, keepdims=True), (rows, tk))

    def weigh(kvh, jt):
        k0 = pl.multiple_of(jt * tk, tk)
        p = jnp.exp(s_ref[kvh, jt] - mb_ref[kvh])
        pt = p[:, :LANES]
        for c in range(1, tk // LANES):
            pt = pt + p[:, c * LANES:(c + 1) * LANES]
        lrun_ref[kvh] = lrun_ref[kvh] + pt
        acc_ref[kvh] = acc_ref[kvh] + _dot(p.astype(BF16), vsd_ref[pl.ds(k0, tk), cols[kvh]])

    def weigh_tile(i, carry):
        for kvh in range(N_KV_HEADS):
            weigh(kvh, tile_ref[i])
        return carry

    lax.fori_loop(0, n_hit, weigh_tile, 0)
    for kvh in range(N_KV_HEADS):
        weigh(kvh, n_full)
    head_out = []
    for kvh in range(N_KV_HEADS):
        gate_col = [jnp.concatenate([gates[:, b * N_HEADS + kvh * GQA + g:b * N_HEADS + kvh * GQA + g + 1]
                                     for g in range(GQA)], axis=0) for b in range(3)]
        pv_win, inv_win = o_win[kvh]
        out = (gate_col[0] * o_cmp[kvh]
               + (gate_col[1] * (1.0 / jnp.sum(lrun_ref[kvh], axis=-1, keepdims=True))) * acc_ref[kvh]
               + (gate_col[2] * inv_win) * pv_win)
        head_out += [out[g * tq:(g + 1) * tq] for g in range(GQA)]
    o = jnp.concatenate([jnp.where(lo, head_out[2 * i], head_out[2 * i + 1]) for i in range(N_HEADS // 2)], axis=1)
    o_ref[...] = _rms_norm(o, go_ref[...]).astype(BF16)


def _nsa_prompt(qpad, gate, ckv, ksa, vsd, kwp, vwd, ov, go, nb, seq, tq=128, tk=256):
    assert seq % tk == 0 and tk % tq == 0 and tq == LANES and seq >= WINDOW + tq and tk // SEL_BLOCK <= 7
    nwin = WINDOW + tq
    nq = seq // tq
    qrow = lambda w: pl.BlockSpec((tq, w), lambda b, i: (b * nq + i, 0))
    seqblk = lambda w: pl.BlockSpec((seq, w), lambda b, i: (b, 0))
    cs = lambda a: pl.BlockSpec(a.shape, lambda b, i: (0,) * a.ndim, pipeline_mode=pl.Buffered(1))
    rows = GQA * tq
    acc = pltpu.VMEM((N_KV_HEADS, rows, LANES), F32)
    return pl.pallas_call(
        functools.partial(_nsa_prompt_body, tq=tq, tk=tk, nwin=nwin),
        grid=(nb, nq),
        in_specs=[qrow(qpad.shape[1]), qrow(LANES), pl.BlockSpec((1,) + ckv.shape[1:], lambda b, i: (b, 0, 0)),
                  seqblk(2 * LANES), seqblk(2 * LANES), seqblk(2 * LANES), seqblk(2 * LANES), cs(ov), cs(go)],
        out_specs=qrow(ATTN_W),
        out_shape=jax.ShapeDtypeStruct((nb * seq, ATTN_W), BF16),
        scratch_shapes=[pltpu.VMEM((N_KV_HEADS, seq // tk, rows, tk), F32), pltpu.VMEM((N_KV_HEADS, rows, LANES), BF16),
                        acc, pltpu.VMEM((N_KV_HEADS, rows, tk), F32), acc, acc,
                        pltpu.SMEM((seq // tk,), jnp.int32)],
        compiler_params=_params(2), name="nsa_prompt",
    )(qpad, gate, ckv, ksa, vsd, kwp, vwd, ov, go)


T8 = 8


def _nsa_sample_body(pt_ref, cache_ref, q_ref, gate_ref, ckv_ref, win_ref, new_ref, e_ref, ov_ref, go_ref,
                     o_ref, buf_ref, sem_ref, *, n_pages, past, n_new, n_cmp):
    def make_copies(seq_idx, slot):
        return [pltpu.make_async_copy(cache_ref.at[pt_ref[seq_idx, j]],
                                      buf_ref.at[slot, :, :, pl.ds(j * PAGE_SIZE, PAGE_SIZE)], sem_ref.at[slot])
                for j in range(n_pages)]

    slot = _gather_step(make_copies)
    rows = N_KV_HEADS * GQA * T8
    ridx = lax.broadcasted_iota(jnp.int32, (rows, 1), 0)
    t_new = ridx & (n_new - 1)
    slope = _alibi_slopes(ridx >> _log2(T8))
    q = q_ref[0]
    new = new_ref[0]

    def tail(x):
        return jnp.concatenate([x, jnp.zeros((LANES - T8, x.shape[1]), F32)], axis=0).astype(BF16)

    def attend(kt, vt, knew, vnew, extra, n_past):
        rel = lax.broadcasted_iota(jnp.int32, (1, n_past + LANES), 1) - n_past
        s = jnp.concatenate([_dot(q, kt) if extra is None else _dot(*extra), _dot_nt(q, knew)], axis=1)
        s = s + slope * rel.astype(F32)
        p = _masked_softmax(s, (rel <= t_new) & (rel >= t_new - WINDOW) & (rel + past >= 0)
                            if extra is None else rel <= t_new)
        return _dot_nt(p[:, :n_past].astype(BF16), vt) + _dot(p[:, n_past:].astype(BF16), vnew)

    nh = ckv_ref.shape[1]
    n_i = lax.broadcasted_iota(jnp.int32, (1, nh), 1)
    dist = (past + t_new) - (n_i * CMP_STRIDE + (CMP_LEN - 1))
    ckv = ckv_ref[0]
    s = _dot_nt(q, ckv[:, :LANES]) - slope * dist.astype(F32)
    pn = _masked_softmax(s, (dist >= 0) & (n_i < n_cmp))
    o_cmp = _dot(pn.astype(BF16), ckv[:, LANES:])
    psum = jnp.concatenate(
        [sum(pn[(k * GQA + g) * T8:(k * GQA + g + 1) * T8] for g in range(GQA)) for k in range(N_KV_HEADS)], axis=0)
    hi, lw = _split_bf16(psum)
    imp = _dot(hi, ov_ref[...]) + _dot(lw, ov_ref[...])
    n_blk = past // SEL_BLOCK
    lane = lax.broadcasted_iota(jnp.int32, (N_KV_HEADS * T8, LANES), 1)
    forced = (lane == 0) | (lane == n_blk - 1)
    score = jnp.where(forced, BIG, jnp.where(lane < n_blk, imp, -BIG))
    cnt = jnp.zeros(score.shape, F32)
    for jp in range(n_blk):
        r = score[:, jp:jp + 1]
        cnt = cnt + _ones_where(r > score) + jnp.where(r == score, _ones_where(lane > jp), 0.0)
    bias = jnp.where(cnt < float(N_SEL - 1), 0.0, NEG).astype(BF16)
    selbias = jnp.concatenate([bias[k * T8:(k + 1) * T8] for k in range(N_KV_HEADS) for _ in range(GQA)], axis=0)
    kt = buf_ref[slot, 0].astype(BF16)
    vt = buf_ref[slot, 1].astype(BF16)
    q_aug = jnp.concatenate([q, selbias], axis=1)
    k_aug = jnp.concatenate([kt, e_ref[...]], axis=0)
    o_slc = attend(kt, vt, tail(new[:, 2 * KV_W:3 * KV_W]), tail(new[:, 3 * KV_W:4 * KV_W]), (q_aug, k_aug), past)
    nw = win_ref.shape[3]
    o_win = attend(win_ref[0, 0].astype(BF16), win_ref[0, 1].astype(BF16),
                   tail(new[:, 4 * KV_W:5 * KV_W]), tail(new[:, 5 * KV_W:6 * KV_W]), None, nw)
    gates = gate_ref[0]
    o = gates[:, 0:1] * o_cmp + gates[:, 1:2] * o_slc + gates[:, 2:3] * o_win
    o_r = pltpu.roll(o, HALF, axis=1)
    lo = lax.broadcasted_iota(jnp.int32, (T8, LANES), 1) < HALF
    groups = []
    for k in range(N_KV_HEADS):
        for i in range(GQA // 2):
            r_even = slice((k * GQA + 2 * i) * T8, (k * GQA + 2 * i + 1) * T8)
            r_odd = slice((k * GQA + 2 * i + 1) * T8, (k * GQA + 2 * i + 2) * T8)
            groups.append(jnp.where(lo, o[r_even], o_r[r_odd]) if k == 0 else jnp.where(lo, o_r[r_even], o[r_odd]))
    o_ref[0] = _rms_norm(jnp.concatenate(groups, axis=1), go_ref[...]).astype(BF16)


def _nsa_sample(page_table, cache_t, q_bd, gate_rows, ckv, win_t, new8, e_mat, ov, go, past, n_new):
    nseq, n_pages = page_table.shape
    assert past == n_pages * PAGE_SIZE and past % SEL_BLOCK == 0 and past // SEL_BLOCK <= LANES
    assert n_new <= T8 and T8 % n_new == 0 and past % CMP_STRIDE == 0
    per_seq = lambda a: pl.BlockSpec((1,) + a.shape[1:], lambda i, pt: (i,) + (0,) * (a.ndim - 1))
    cs = lambda a: pl.BlockSpec(a.shape, lambda i, pt: (0,) * a.ndim, pipeline_mode=pl.Buffered(1))
    return pl.pallas_call(
        functools.partial(_nsa_sample_body, n_pages=n_pages, past=past, n_new=n_new,
                          n_cmp=past // CMP_STRIDE - 1),
        grid_spec=pltpu.PrefetchScalarGridSpec(
            num_scalar_prefetch=1, grid=(nseq,),
            in_specs=[pl.BlockSpec(memory_space=pl.ANY), per_seq(q_bd), per_seq(gate_rows), per_seq(ckv),
                      per_seq(win_t), per_seq(new8), cs(e_mat), cs(ov), cs(go)],
            out_specs=pl.BlockSpec((1, T8, ATTN_W), lambda i, pt: (i, 0, 0)),
            scratch_shapes=[pltpu.VMEM((2, 2, KV_W, past), F32), pltpu.SemaphoreType.DMA((2,))]),
        out_shape=jax.ShapeDtypeStruct((nseq, T8, ATTN_W), BF16),
        compiler_params=_params(1), name="nsa_sample",
    )(page_table, cache_t, q_bd, gate_rows, ckv, win_t, new8, e_mat, ov, go)


def _overlap(n_rows, n_cmp, lane0):
    n = np.arange(n_rows)[:, None]
    j = np.arange(LANES)[None, :] - lane0
    c_end = n * CMP_STRIDE + (CMP_LEN - 1)
    ov = (j * SEL_BLOCK <= c_end) & (j * SEL_BLOCK + SEL_BLOCK > c_end + 1 - CMP_LEN) & (n < n_cmp) & (j >= 0)
    return jnp.asarray(ov, BF16)


def _compress_weights(k_pos, k_w1, k_b1, k_w2, v_pos, v_w1, v_b1, v_w2):
    eye = jnp.eye(4, dtype=F32)
    feat = 4 * HEAD_DIM
    w1 = jnp.stack([k_w1, k_w1, v_w1, v_w1])
    half = lambda a: jnp.einsum("cldh,ce->lcdeh", a, eye).reshape(CMP_STRIDE, feat, 4 * CMP_HIDDEN)
    wab = jnp.concatenate([half(w1[:, :CMP_STRIDE]), half(w1[:, CMP_STRIDE:])], axis=2).astype(BF16)
    pos = jnp.stack([k_pos, k_pos, v_pos, v_pos])
    flat = lambda a: jnp.broadcast_to(jnp.transpose(a, (1, 0, 2)).reshape(CMP_STRIDE, 1, feat), (CMP_STRIDE, 8, feat))
    pos16 = jnp.concatenate([flat(pos[:, :CMP_STRIDE]), flat(pos[:, CMP_STRIDE:])], axis=1).astype(BF16)
    b1 = jnp.concatenate([k_b1, k_b1, v_b1, v_b1]).reshape(1, -1)
    z = jnp.zeros((CMP_HIDDEN, HEAD_DIM), F32)
    w2p = jnp.concatenate([
        jnp.concatenate([k_w2, z, z, z, z, z, z, z], axis=1), jnp.concatenate([z, z, k_w2, z, z, z, z, z], axis=1),
        jnp.concatenate([z, z, z, z, v_w2, v_w2, z, z], axis=1), jnp.concatenate([z, z, z, z, z, z, v_w2, v_w2], axis=1),
    ], axis=0).astype(BF16)
    w2s = jnp.concatenate([
        jnp.concatenate([k_w2, z, z, z], axis=1), jnp.concatenate([z, k_w2, z, z], axis=1),
        jnp.concatenate([z, z, v_w2, z], axis=1), jnp.concatenate([z, z, z, v_w2], axis=1)], axis=0).astype(BF16)
    return wab, pos16, b1, w2p, w2s


def _proj_weight(w_in):
    d = w_in.shape[0]
    g0 = ATTN_W + 6 * KV_W
    wg = jnp.concatenate([w_in[:, g0:g0 + 3 * N_HEADS], jnp.zeros((d, LANES - 3 * N_HEADS), F32)], axis=1)
    return jnp.concatenate([w_in[:, :ATTN_W] * HEAD_DIM ** -0.5, w_in[:, ATTN_W:g0], wg,
                            w_in[:, g0 + 3 * N_HEADS:]], axis=1).astype(BF16)


def _feature_major(a):
    lead = a.shape[:-4]
    n = len(lead)
    a = jnp.transpose(a, tuple(range(n)) + (n + 1, n + 2, n + 3, n))
    return a.reshape(lead + (2, KV_W, a.shape[-1]))


def _token_major(a, nb):
    return jnp.transpose(a.reshape(nb, 2, N_KV_HEADS, HEAD_DIM, a.shape[-1]), (0, 4, 1, 2, 3))


def kernel(x_prompt, x_sample, cache_cmp_kv, cache_slc_kv, state_win_kv, page_table, ln1_g, ln1_b, ffn1_w_in, ffn1_w_out, w_in, cmp_k_pos, cmp_k_w1, cmp_k_b1, cmp_k_w2, cmp_v_pos, cmp_v_w1, cmp_v_b1, cmp_v_w2, sgu_ln_g, sgu_ln_b, sgu_w, sgu_b, out_norm_g, w_out, ln2_g, ln2_b, ffn2_w_in, ffn2_w_out, ln3_g, ln3_b):
    depth = w_in.shape[0]
    nb, seq, d = x_prompt.shape
    ns, n_new, _ = x_sample.shape
    n_pages = page_table.shape[1]
    past = n_pages * PAGE_SIZE
    d_ff = ffn1_w_out.shape[1]
    alpha = (2.0 * depth) ** 0.25
    row = lambda a: a.reshape(1, -1)
    hp = x_prompt.reshape(nb * seq, d)
    hs = x_sample.reshape(ns * n_new, d)
    kv5 = (N_KV_HEADS, HEAD_DIM)
    outs = [[] for _ in range(7)]
    ov_p = _overlap(seq // CMP_STRIDE, seq // CMP_STRIDE - 1, HALF)
    ov_s = _overlap(past // CMP_STRIDE, past // CMP_STRIDE - 1, 0)
    e_mat = jnp.asarray(np.arange(LANES)[:, None] == np.arange(past)[None, :] // SEL_BLOCK, BF16)
    for l in range(depth):
        f1 = (ffn1_w_in[l, :, :d_ff].astype(BF16), ffn1_w_in[l, :, d_ff:].astype(BF16), ffn1_w_out[l].astype(BF16),
              row(ln1_g[l]), row(ln1_b[l]))
        f2 = (ffn2_w_in[l, :, :d_ff].astype(BF16), ffn2_w_in[l, :, d_ff:].astype(BF16), ffn2_w_out[l].astype(BF16),
              row(ln3_g[l]), row(ln3_b[l]))
        wp = _proj_weight(w_in[l])
        wab, pos16, b1, w2p, w2s = _compress_weights(cmp_k_pos[l], cmp_k_w1[l], cmp_k_b1[l], cmp_k_w2[l],
                                                     cmp_v_pos[l], cmp_v_w1[l], cmp_v_b1[l], cmp_v_w2[l])
        sg, sb = row(sgu_ln_g[l]), row(sgu_ln_b[l])
        go_a, go_g = row(out_norm_g[l, :ATTN_W]), row(out_norm_g[l, ATTN_W:])
        wm = w_out[l].astype(BF16)
        mix_p = jnp.tril(sgu_w[l]).astype(BF16)
        mb_p = jnp.repeat(sgu_b[l].T, GMLP_DIM, axis=1)
        r = min(n_new, CHUNK)
        mix_s = jax.vmap(lambda w: jnp.kron(jnp.eye(CHUNK // r, dtype=F32), w))(jnp.tril(sgu_w[l, :, :r, :r])).astype(BF16)
        mb_s = jnp.tile(jnp.repeat(sgu_b[l, :, :r].T, GMLP_DIM, axis=1), (CHUNK // r, 1))

        hp = _ffn_ln(hp, *f1, alpha)
        (qpad, cmpk_p, cmpv_p, _, _, cmp_t, slc_t, win_t, ksa, vsd, kwp, vwd, gate, ogn, _) = _proj(
            hp, wp, sg, sb, mix_p, mb_p, go_g, seq)
        ckv = _compress_prompt(cmpk_p, cmpv_p, nb, wab, pos16, b1, w2p)
        oan = _nsa_prompt(qpad, gate, ckv, ksa, vsd, kwp, vwd, ov_p, go_a, nb, seq)
        hp = _ffn_ln(hp, *f2, alpha, merge_args=(oan, ogn, wm, row(ln2_g[l]), row(ln2_b[l])))
        outs[0].append(_token_major(cmp_t, nb))
        outs[2].append(_token_major(slc_t, nb))
        outs[4].append(_token_major(win_t[:, :, seq - min(WINDOW, seq):], nb))

        hs = _ffn_ln(hs, *f1, alpha)
        (qpad, cmpk_s, cmpv_s, slc_s, win_s, _, _, _, _, _, _, _, gate, ogn, v_s) = _proj(
            hs, wp, sg, sb, mix_s, mb_s, go_g, ns * n_new)
        cmp_s = jnp.concatenate([cmpk_s, cmpv_s], axis=1)
        ckv = _compress_sample(page_table, _feature_major(cache_cmp_kv[l]), wab, pos16, b1, w2s)
        q5 = qpad.reshape(ns, n_new, N_KV_HEADS, GQA, LANES)[..., :HEAD_DIM]
        q5 = jnp.pad(jnp.transpose(q5, (0, 2, 3, 1, 4)), ((0, 0), (0, 0), (0, 0), (0, T8 - n_new), (0, 0)))
        zq = jnp.zeros_like(q5[:, 0])
        q_bd = jnp.stack([jnp.concatenate([q5[:, 0], zq], axis=-1), jnp.concatenate([zq, q5[:, 1]], axis=-1)],
                         axis=1).reshape(ns, N_KV_HEADS * GQA * T8, LANES)
        g5 = gate[:, :3 * N_HEADS].reshape(ns, n_new, 3, N_KV_HEADS, GQA)
        g5 = jnp.pad(jnp.transpose(g5, (0, 3, 4, 1, 2)), ((0, 0), (0, 0), (0, 0), (0, T8 - n_new), (0, 0)))
        gate_rows = g5.reshape(ns, N_KV_HEADS * GQA * T8, 3)
        new8 = jnp.pad(jnp.concatenate([cmp_s, slc_s, win_s], axis=1).reshape(ns, n_new, 6 * KV_W),
                       ((0, 0), (0, T8 - n_new), (0, 0)))
        oan = _nsa_sample(page_table, _feature_major(cache_slc_kv[l]), q_bd, gate_rows, ckv,
                          _feature_major(state_win_kv[l]), new8, e_mat, ov_s, go_a, past, n_new)
        oan = oan[:, :n_new].reshape(ns * n_new, ATTN_W)
        hs = _ffn_ln(hs, *f2, alpha, merge_args=(oan, ogn, wm, row(ln2_g[l]), row(ln2_b[l])))
        win_all = jnp.concatenate([state_win_kv[l], win_s.reshape(ns, n_new, 2, *kv5)], axis=1)
        outs[1].append(cmp_s.reshape(ns, n_new, 2, *kv5))
        outs[3].append(slc_s.reshape(ns, n_new, 2, *kv5))
        outs[5].append(win_all[:, -min(WINDOW, past + n_new):])
        outs[6].append(v_s.reshape(ns, n_new, N_GMLP, GMLP_DIM))
    st = [jnp.stack(o, axis=0) for o in outs]
    return (hp.reshape(nb, seq, d), hs.reshape(ns, n_new, d), st[0], st[1], st[2], st[3], st[4], st[5], st[6])
```

```python
import functools

import numpy as np
import jax
import jax.numpy as jnp
from jax import lax
from jax.experimental import pallas as pl
from jax.experimental.pallas import tpu as pltpu

HEAD_DIM = 64
N_HEADS = 8
N_KV_HEADS = 2
GQA = N_HEADS // N_KV_HEADS
ATTN_W = N_HEADS * HEAD_DIM
KV_W = N_KV_HEADS * HEAD_DIM
CMP_LEN = 32
CMP_STRIDE = 16
CMP_HIDDEN = 64
SEL_BLOCK = 64
N_SEL = 16
WINDOW = 512
N_GMLP = 8
GMLP_DIM = 64
GMLP_W = N_GMLP * GMLP_DIM
CHUNK = 128
PAGE_SIZE = 128
LN_EPS = 1e-5
NEG = -1e30
BIG = 1e9

LANES = 128
HALF = LANES // 2
VMEM_LIMIT = 56 * 1024 * 1024
SLOPES = [2.0 ** -(h + 1) for h in range(N_HEADS)]

F32 = jnp.float32
BF16 = jnp.bfloat16

_Q0, _KV0, _G0, _U0, _V0, _PW = 0, ATTN_W, ATTN_W + 6 * KV_W, ATTN_W + 6 * KV_W + LANES, \
    ATTN_W + 6 * KV_W + LANES + GMLP_W, ATTN_W + 6 * KV_W + LANES + 2 * GMLP_W


def _dot(a, b):
    return jnp.dot(a, b, preferred_element_type=F32)


def _dot_nt(a, b):
    return lax.dot_general(a, b, (((1,), (1,)), ((), ())), preferred_element_type=F32)


def _layer_norm(y, g, b):
    mu = jnp.mean(y, axis=-1, keepdims=True)
    d = y - mu
    var = jnp.mean(d * d, axis=-1, keepdims=True)
    return d * lax.rsqrt(var + LN_EPS) * g + b


def _rms_norm(y, g):
    return y * lax.rsqrt(jnp.mean(y * y, axis=-1, keepdims=True) + LN_EPS) * g


def _ones_where(mask):
    return jnp.where(mask, 1.0, 0.0)


def _alibi_lanes(lane, hi, lo):
    return jnp.where(lane == HALF, hi, jnp.where(lane == HALF + 1, lo, 0.0))


def _log2(n):
    assert n > 0 and n & (n - 1) == 0, n
    return n.bit_length() - 1


def _const_spec(shape):
    nd = len(shape)
    return pl.BlockSpec(shape, lambda *_: (0,) * nd, pipeline_mode=pl.Buffered(1))


def _params(n_grid):
    return pltpu.CompilerParams(dimension_semantics=("arbitrary",) * n_grid, vmem_limit_bytes=VMEM_LIMIT)


def _ffn_body(*refs, alpha, merge):
    if merge:
        (h_ref, oa_ref, og_ref, wm_ref, gm_ref, bm_ref,
         wa_ref, wb_ref, wo_ref, g_ref, b_ref, o_ref) = refs
        cat = jnp.concatenate([oa_ref[...], og_ref[...]], axis=1)
        x = _layer_norm(alpha * h_ref[...] + _dot(cat, wm_ref[...]), gm_ref[...], bm_ref[...])
    else:
        x_ref, wa_ref, wb_ref, wo_ref, g_ref, b_ref, o_ref = refs
        x = x_ref[...]
    xb = x.astype(BF16)
    a = _dot(xb, wa_ref[...])
    b = _dot(xb, wb_ref[...])
    hid = ((a * jax.nn.sigmoid(a)) * b).astype(BF16)
    y = alpha * x + 0.5 * _dot(hid, wo_ref[...])
    o_ref[...] = _layer_norm(y, g_ref[...], b_ref[...])


def _ffn_ln(x, wa, wb, wo, g, b, alpha, merge_args=None, tm=512):
    t, d = x.shape
    tm = min(tm, t)
    assert t % tm == 0
    row = lambda w: pl.BlockSpec((tm, w), lambda i: (i, 0))
    ins, specs = [x], [row(d)]
    if merge_args is not None:
        oa, og, wm, gm, bm = merge_args
        ins += [oa, og, wm, gm, bm]
        specs += [row(oa.shape[1]), row(og.shape[1]), _const_spec(wm.shape), _const_spec(gm.shape),
                  _const_spec(bm.shape)]
    ins += [wa, wb, wo, g, b]
    specs += [_const_spec(a.shape) for a in (wa, wb, wo, g, b)]
    return pl.pallas_call(
        functools.partial(_ffn_body, alpha=alpha, merge=merge_args is not None),
        grid=(t // tm,), in_specs=specs, out_specs=row(d),
        out_shape=jax.ShapeDtypeStruct((t, d), F32), compiler_params=_params(1),
        name="ffn_merge_ln" if merge_args is not None else "ffn_ln",
    )(*ins)


def _proj_body(h_ref, w_ref, sg_ref, sb_ref, mix_ref, mb_ref, go_ref,
               qpad_ref, cmpk_ref, cmpv_ref, slc_ref, win_ref, cmpt_ref, slct_ref, wint_ref,
               ksa_ref, vsd_ref, kwp_ref, vwd_ref, gate_ref, ogn_ref, v_ref, *, tm, seq):
    p = _dot(h_ref[...].astype(BF16), w_ref[...])
    lane = lax.broadcasted_iota(jnp.int32, (tm, LANES), 1)
    lo = lane < HALF
    tok = lax.rem(pl.program_id(0) * tm, seq) + lax.broadcasted_iota(jnp.int32, (tm, LANES), 0)
    for i in range(ATTN_W // LANES):
        x = p[:, _Q0 + i * LANES:_Q0 + (i + 1) * LANES]
        xr = pltpu.roll(x, HALF, axis=1)
        qpad_ref[:, (2 * i) * LANES:(2 * i + 1) * LANES] = jnp.where(
            lo, x, _alibi_lanes(lane, LANES * SLOPES[2 * i], SLOPES[2 * i])).astype(BF16)
        qpad_ref[:, (2 * i + 1) * LANES:(2 * i + 2) * LANES] = jnp.where(
            lo, xr, _alibi_lanes(lane, LANES * SLOPES[2 * i + 1], SLOPES[2 * i + 1])).astype(BF16)
    for s, feat_ref in enumerate((cmpt_ref, slct_ref, wint_ref)):
        feat_ref[0] = p[:, _KV0 + 2 * s * KV_W:_KV0 + 2 * (s + 1) * KV_W].T
    cmpk_ref[...] = p[:, _KV0:_KV0 + KV_W]
    cmpv_ref[...] = p[:, _KV0 + KV_W:_KV0 + 2 * KV_W]
    slc_ref[...] = p[:, _KV0 + 2 * KV_W:_KV0 + 4 * KV_W]
    win_ref[...] = p[:, _KV0 + 4 * KV_W:_KV0 + 6 * KV_W]
    onehot = _ones_where(lane - HALF == (tok >> _log2(SEL_BLOCK)))
    ks = p[:, _KV0 + 2 * KV_W:_KV0 + 3 * KV_W]
    ksr = pltpu.roll(ks, HALF, axis=1)
    ksa_ref[:, :LANES] = jnp.where(lo, ks, onehot).astype(BF16)
    ksa_ref[:, LANES:] = jnp.where(lo, ksr, onehot).astype(BF16)
    vs = p[:, _KV0 + 3 * KV_W:_KV0 + 4 * KV_W]
    vsr = pltpu.roll(vs, HALF, axis=1)
    vsd_ref[:, :LANES] = jnp.where(lo, vs, 1.0).astype(BF16)
    vsd_ref[:, LANES:] = jnp.where(lo, vsr, 1.0).astype(BF16)
    kw = p[:, _KV0 + 4 * KV_W:_KV0 + 5 * KV_W]
    kwr = pltpu.roll(kw, HALF, axis=1)
    pos_pair = _alibi_lanes(lane, (tok >> _log2(LANES)).astype(F32), (tok & (LANES - 1)).astype(F32))
    kwp_ref[:, :LANES] = jnp.where(lo, kw, pos_pair).astype(BF16)
    kwp_ref[:, LANES:] = jnp.where(lo, kwr, pos_pair).astype(BF16)
    vw = p[:, _KV0 + 5 * KV_W:_KV0 + 6 * KV_W]
    vwr = pltpu.roll(vw, HALF, axis=1)
    vwd_ref[:, :LANES] = jnp.where(lo, vw, 1.0).astype(BF16)
    vwd_ref[:, LANES:] = jnp.where(lo, vwr, 1.0).astype(BF16)
    gate_ref[...] = jax.nn.sigmoid(p[:, _G0:_G0 + LANES])
    u = jax.nn.gelu(p[:, _U0:_U0 + GMLP_W])
    v = _layer_norm(jax.nn.gelu(p[:, _V0:_V0 + GMLP_W]), sg_ref[...], sb_ref[...])
    v_ref[...] = v
    vb = v.astype(BF16)
    lo_c = lax.broadcasted_iota(jnp.int32, (CHUNK, LANES), 1) < HALF
    for c in range(tm // CHUNK):
        rows = slice(c * CHUNK, (c + 1) * CHUNK)
        zs = []
        for i in range(GMLP_W // LANES):
            vp = vb[rows, i * LANES:(i + 1) * LANES]
            zs.append(jnp.where(lo_c, _dot(mix_ref[2 * i], vp), _dot(mix_ref[2 * i + 1], vp)))
        z = jnp.concatenate(zs, axis=1) + mb_ref[...]
        ogn_ref[rows, :] = _rms_norm(u[rows, :] * z, go_ref[...]).astype(BF16)


def _proj(h, w, sg, sb, mix, mb, go, seq, tm=512):
    t, d = h.shape
    tm = min(tm, t)
    assert t % tm == 0 and tm % CHUNK == 0 and seq % tm == 0 and t % seq == 0 and seq // SEL_BLOCK <= HALF
    nq = seq // tm
    row = lambda width: pl.BlockSpec((tm, width), lambda i: (i, 0))
    feat = pl.BlockSpec((1, 2 * KV_W, tm), lambda i: (i // nq, 0, i % nq))
    tok_major = [(2 * ATTN_W, BF16), (KV_W, F32), (KV_W, F32), (2 * KV_W, F32), (2 * KV_W, F32)]
    rest = [(2 * LANES, BF16), (2 * LANES, BF16), (2 * LANES, BF16), (2 * LANES, BF16), (LANES, F32),
            (GMLP_W, BF16), (GMLP_W, F32)]
    sds = jax.ShapeDtypeStruct
    return pl.pallas_call(
        functools.partial(_proj_body, tm=tm, seq=seq),
        grid=(t // tm,),
        in_specs=[row(d)] + [_const_spec(a.shape) for a in (w, sg, sb, mix, mb, go)],
        out_specs=[row(wd) for wd, _ in tok_major] + [feat] * 3 + [row(wd) for wd, _ in rest],
        out_shape=[sds((t, wd), dt) for wd, dt in tok_major] + [sds((t // seq, 2 * KV_W, seq), F32)] * 3
        + [sds((t, wd), dt) for wd, dt in rest],
        compiler_params=_params(1), name="proj",
    )(h, w, sg, sb, mix, mb, go)


def _compress_math(rows_at, nh, wab_ref, pos_ref, b1_ref, w2_ref):
    hw = b1_ref.shape[1]
    ab = None
    for l in range(CMP_STRIDE):
        lhs = jnp.concatenate([rows_at(l).astype(BF16), pos_ref[l]], axis=0)
        d = _dot(lhs, wab_ref[l])
        ab = d if ab is None else ab + d
    c = ab[nh:nh + 1, :hw] + ab[nh + 8:nh + 9, hw:] + b1_ref[...]
    hid = ab[:nh, :hw] + pltpu.roll(ab[:nh, hw:], nh - 1, axis=0) + c
    return _dot(jax.nn.gelu(hid).astype(BF16), w2_ref[...])


def _strided_rows(k_ref, v_ref, nh):
    return lambda l: jnp.concatenate([k_ref[pl.ds(l, nh, stride=CMP_STRIDE), :],
                                      v_ref[pl.ds(l, nh, stride=CMP_STRIDE), :]], axis=1)


def _compress_body(xk_ref, xv_ref, wab_ref, pos_ref, b1_ref, w2_ref, o_ref):
    nh = o_ref.shape[1]
    out = _compress_math(_strided_rows(xk_ref, xv_ref, nh), nh, wab_ref, pos_ref, b1_ref, w2_ref)
    lane = lax.broadcasted_iota(jnp.int32, (nh, LANES), 1)
    c_end = lax.broadcasted_iota(jnp.int32, (nh, LANES), 0) * CMP_STRIDE + (CMP_LEN - 1)
    pair = _alibi_lanes(lane, (c_end >> _log2(LANES)).astype(F32), (c_end & (LANES - 1)).astype(F32))
    for k in range(N_KV_HEADS):
        o_ref[0, :, k * LANES:(k + 1) * LANES] = (out[:, k * LANES:(k + 1) * LANES] + pair).astype(o_ref.dtype)
    o_ref[0, :, N_KV_HEADS * LANES:] = out[:, N_KV_HEADS * LANES:].astype(o_ref.dtype)


def _compress_prompt(xk, xv, nb, wab, pos, b1, w2):
    seq = xk.shape[0] // nb
    nh = seq // CMP_STRIDE
    return pl.pallas_call(
        _compress_body, grid=(nb,),
        in_specs=[pl.BlockSpec((seq, KV_W), lambda i: (i, 0))] * 2 + [_const_spec(a.shape) for a in (wab, pos, b1, w2)],
        out_specs=pl.BlockSpec((1, nh, w2.shape[1]), lambda i: (i, 0, 0)),
        out_shape=jax.ShapeDtypeStruct((nb, nh, w2.shape[1]), BF16),
        compiler_params=_params(1), name="compress_prompt",
    )(xk, xv, wab, pos, b1, w2)


def _gather_step(make_copies):
    i, n = pl.program_id(0), pl.num_programs(0)
    slot = lax.rem(i, 2)

    @pl.when(i == 0)
    def _():
        for cp in make_copies(i, slot):
            cp.start()

    @pl.when(i + 1 < n)
    def _():
        for cp in make_copies(i + 1, 1 - slot):
            cp.start()

    for cp in make_copies(i, slot):
        cp.wait()
    return slot


def _compress_sample_body(pt_ref, cache_ref, perm_ref, wab_ref, pos_ref, b1_ref, w2_ref, o_ref, buf_ref, xs_ref,
                          sem_ref, *, n_pages):
    def make_copies(seq_idx, slot):
        return [pltpu.make_async_copy(cache_ref.at[pt_ref[seq_idx, j]], buf_ref.at[slot, j], sem_ref.at[slot])
                for j in range(n_pages)]

    slot = _gather_step(make_copies)
    per_page = PAGE_SIZE // CMP_STRIDE

    feat, page = buf_ref.shape[3], buf_ref.shape[4]
    xt = buf_ref[slot].reshape(n_pages * 2 * feat, page)
    z = _dot_nt(perm_ref[...], xt.astype(BF16))
    for j in range(n_pages):
        for c in range(2):
            col = slice((2 * j + c) * feat, (2 * j + c + 1) * feat)
            for l in range(CMP_STRIDE):
                xs_ref[c, l, j * per_page:(j + 1) * per_page, :] = z[l * per_page:(l + 1) * per_page, col]
    nh = o_ref.shape[1]
    rows_at = lambda l: jnp.concatenate([xs_ref[0, l], xs_ref[1, l]], axis=1)
    o_ref[0] = _compress_math(rows_at, nh, wab_ref, pos_ref, b1_ref, w2_ref).astype(o_ref.dtype)


def _compress_sample(page_table, cache_t, wab, pos, b1, w2):
    nseq, n_pages = page_table.shape
    _, _, feat, page = cache_t.shape
    nh = n_pages * page // CMP_STRIDE
    cs = lambda a: pl.BlockSpec(a.shape, lambda i, pt: (0,) * a.ndim, pipeline_mode=pl.Buffered(1))
    tok = np.arange(page)
    perm = jnp.asarray((tok % CMP_STRIDE * (page // CMP_STRIDE) + tok // CMP_STRIDE)[None, :] == tok[:, None], BF16)
    return pl.pallas_call(
        functools.partial(_compress_sample_body, n_pages=n_pages),
        grid_spec=pltpu.PrefetchScalarGridSpec(
            num_scalar_prefetch=1, grid=(nseq,),
            in_specs=[pl.BlockSpec(memory_space=pl.ANY)] + [cs(a) for a in (perm, wab, pos, b1, w2)],
            out_specs=pl.BlockSpec((1, nh, w2.shape[1]), lambda i, pt: (i, 0, 0)),
            scratch_shapes=[pltpu.VMEM((2, n_pages, 2, feat, page), F32), pltpu.VMEM((2, CMP_STRIDE, nh, feat), F32),
                            pltpu.SemaphoreType.DMA((2,))]),
        out_shape=jax.ShapeDtypeStruct((nseq, nh, w2.shape[1]), BF16),
        compiler_params=_params(1), name="compress_sample",
    )(page_table, cache_t, perm, wab, pos, b1, w2)


def _masked_softmax(s, mask):
    sm = jnp.where(mask, s, NEG)
    m = jnp.max(sm, axis=-1, keepdims=True)
    p = jnp.where(mask, jnp.exp(sm - m), 0.0)
    l = jnp.sum(p, axis=-1, keepdims=True)
    return p * (1.0 / jnp.maximum(l, 1e-30))


def _split_bf16(x):
    hi = x.astype(BF16)
    return hi, (x - hi.astype(F32)).astype(BF16)


def _alibi_slopes(head_idx):
    out = jnp.zeros(head_idx.shape, F32)
    for h in range(N_HEADS):
        out = jnp.where(head_idx == h, SLOPES[h], out)
    return out


def _nsa_prompt_body(q_ref, gate_ref, ckv_ref, ksa_ref, vsd_ref, kwp_ref, vwd_ref, ov_ref, go_ref,
                     o_ref, s_ref, qsel_ref, mrun_ref, mb_ref, acc_ref, tile_ref, *, tq, tk, nwin):
    n_tiles = s_ref.shape[1]
    t0 = pl.program_id(1) * tq
    rows = GQA * tq
    lane = lax.broadcasted_iota(jnp.int32, (tq, LANES), 1)
    lo = lane < HALF
    t_rel = lax.broadcasted_iota(jnp.int32, (rows, 1), 0) & (tq - 1)
    cur = (t0 + lax.broadcasted_iota(jnp.int32, (tq, 1), 0)) >> _log2(SEL_BLOCK)
    sub = lax.broadcasted_iota(jnp.int32, (8, tq), 0)
    gates = gate_ref[...]
    nh = ckv_ref.shape[1]
    c_rel = lax.broadcasted_iota(jnp.int32, (1, nh), 1) * CMP_STRIDE + (CMP_LEN - 1) - t0
    n_full = t0 // tk
    w_start = pl.multiple_of(jnp.maximum(t0 - WINDOW, 0), tq)
    w_rel = (w_start - t0) + lax.broadcasted_iota(jnp.int32, (1, nwin), 1)
    t_q = lax.broadcasted_iota(jnp.int32, (tq, 1), 0)
    w_bias = jnp.where((w_rel <= t_q) & (w_rel >= t_q - WINDOW), 0.0, NEG)
    cols = [slice(k * LANES, (k + 1) * LANES) for k in range(N_KV_HEADS)]
    blk_tile = lax.broadcasted_iota(jnp.int32, (HALF, 1), 0) >> _log2(tk // SEL_BLOCK)

    def add_alibi(s, kvh, rel):
        relf = rel.astype(F32)
        return jnp.concatenate([s[g * tq:(g + 1) * tq] + SLOPES[kvh * GQA + g] * relf for g in range(GQA)], axis=0)

    o_cmp, o_win = [], []
    tile_hits = [jnp.int32(0)] * n_tiles
    for kvh in range(N_KV_HEADS):
        qs = [q_ref[:, (kvh * GQA + g) * LANES:(kvh * GQA + g + 1) * LANES] for g in range(GQA)]
        qall = jnp.concatenate(qs, axis=0)
        pn = _masked_softmax(_dot_nt(qall, ckv_ref[0, :, cols[kvh]]), c_rel <= t_rel)
        o_cmp.append(_dot(pn.astype(BF16), ckv_ref[0, :, 2 * LANES + kvh * LANES:2 * LANES + (kvh + 1) * LANES]))
        psum = pn[0:tq] + pn[tq:2 * tq] + pn[2 * tq:3 * tq] + pn[3 * tq:4 * tq]
        hi, lw = _split_bf16(psum)
        imp = _dot(hi, ov_ref[...]) + _dot(lw, ov_ref[...])
        j = lane - HALF
        forced = (j == 0) | (j == cur) | (j == cur - 1)
        score = jnp.where(forced, BIG, jnp.where(j <= cur, imp, -BIG))
        sc_t = score.T[HALF:, :]
        cnt = [jnp.zeros((8, tq), F32) for _ in range(HALF // 8)]
        for jp in range(HALF):
            r = sc_t[jp:jp + 1, :]
            for v in range(HALF // 8):
                blk = sc_t[8 * v:8 * v + 8, :]
                if 8 * v > jp:
                    beats = _ones_where(r >= blk)
                elif 8 * v + 7 <= jp:
                    beats = _ones_where(r > blk)
                else:
                    beats = _ones_where(r > blk) + jnp.where(r == blk, _ones_where(sub > jp - 8 * v), 0.0)
                cnt[v] = cnt[v] + beats
        sel = jnp.concatenate(cnt, axis=0) < float(N_SEL)
        selbias = jnp.concatenate([jnp.zeros((HALF, tq), F32), jnp.where(sel, 0.0, NEG)], axis=0).T.astype(BF16)
        qsel_ref[kvh] = jnp.concatenate([jnp.where(lo, q, selbias) for q in qs], axis=0)
        any_sel = jnp.max(_ones_where(sel), axis=-1, keepdims=True)
        for w in range(-(-n_tiles // 8)):
            weight = jnp.zeros((HALF, 1), F32)
            for k in range(8):
                weight = jnp.where(blk_tile == 8 * w + k, float(8 ** k), weight)
            packed = jnp.sum(any_sel * weight).astype(jnp.int32)
            for k in range(min(8, n_tiles - 8 * w)):
                tile_hits[8 * w + k] = tile_hits[8 * w + k] + ((packed >> (3 * k)) & 7)
        s = _dot_nt(qall, kwp_ref[pl.ds(w_start, nwin), cols[kvh]])
        s = jnp.concatenate([s[g * tq:(g + 1) * tq] + w_bias for g in range(GQA)], axis=0)
        pw = jnp.exp((s - jnp.max(s, axis=-1, keepdims=True)).astype(BF16))
        o_win.append(_dot(pw, vwd_ref[pl.ds(w_start, nwin), cols[kvh]]))
        mrun_ref[kvh] = jnp.full((rows, LANES), NEG, F32)
        acc_ref[kvh] = jnp.zeros((rows, LANES), F32)

    def scores(kvh, jt, causal):
        k0 = pl.multiple_of(jt * tk, tk)
        k_rel = (k0 - t0) + lax.broadcasted_iota(jnp.int32, (1, tk), 1)
        s = add_alibi(_dot_nt(qsel_ref[kvh], ksa_ref[pl.ds(k0, tk), cols[kvh]]), kvh, k_rel)
        if causal:
            s = jnp.where(k_rel <= t_rel, s, NEG)
        s_ref[kvh, jt] = s
        mt = s[:, :LANES]
        for c in range(1, tk // LANES):
            mt = jnp.maximum(mt, s[:, c * LANES:(c + 1) * LANES])
        mrun_ref[kvh] = jnp.maximum(mrun_ref[kvh], mt)

    n_hit = jnp.int32(0)
    for jt in range(n_tiles):
        tile_ref[n_hit] = jt
        n_hit = n_hit + ((tile_hits[jt] > 0) & (jt < n_full)).astype(jnp.int32)

    def sweep(visit):
        def pair(i, carry):
            for u in range(2):
                for kvh in range(N_KV_HEADS):
                    visit(kvh, tile_ref[2 * i + u])
            return carry

        lax.fori_loop(0, n_hit >> 1, pair, 0)

        @pl.when((n_hit & 1) == 1)
        def _():
            for kvh in range(N_KV_HEADS):
                visit(kvh, tile_ref[n_hit - 1])

    sweep(lambda kvh, jt: scores(kvh, jt, False))
    for kvh in range(N_KV_HEADS):
        scores(kvh, n_full, True)
        mb_ref[kvh] = jnp.broadcast_to(jnp.max(mrun_ref[kvh], axis=-1, keepdims=True), (rows, tk))

    def weigh(kvh, jt):
        k0 = pl.multiple_of(jt * tk, tk)
        p = jnp.exp((s_ref[kvh, jt] - mb_ref[kvh]).astype(BF16))
        acc_ref[kvh] = acc_ref[kvh] + _dot(p, vsd_ref[pl.ds(k0, tk), cols[kvh]])

    sweep(weigh)
    for kvh in range(N_KV_HEADS):
        weigh(kvh, n_full)
    head_out = []
    for kvh in range(N_KV_HEADS):
        gate_col = [jnp.concatenate([gates[:, b * N_HEADS + kvh * GQA + g:b * N_HEADS + kvh * GQA + g + 1]
                                     for g in range(GQA)], axis=0) for b in range(3)]
        pv_slc, pv_win = acc_ref[kvh], o_win[kvh]
        out = (gate_col[0] * o_cmp[kvh]
               + (gate_col[1] * (1.0 / pv_slc[:, HALF:HALF + 1])) * pv_slc
               + (gate_col[2] * (1.0 / pv_win[:, HALF:HALF + 1])) * pv_win)
        head_out += [out[g * tq:(g + 1) * tq] for g in range(GQA)]
    o = jnp.concatenate([jnp.where(lo, head_out[2 * i], pltpu.roll(head_out[2 * i + 1], HALF, axis=1))
                         for i in range(N_HEADS // 2)], axis=1)
    o_ref[...] = _rms_norm(o, go_ref[...]).astype(BF16)


def _nsa_prompt(qpad, gate, ckv, ksa, vsd, kwp, vwd, ov, go, nb, seq, tq=128, tk=256):
    assert seq % tk == 0 and tk % tq == 0 and tq == LANES and seq >= WINDOW + tq and tk // SEL_BLOCK <= 7
    nwin = WINDOW + tq
    nq = seq // tq
    qrow = lambda w: pl.BlockSpec((tq, w), lambda b, i: (b * nq + i, 0))
    seqblk = lambda w: pl.BlockSpec((seq, w), lambda b, i: (b, 0))
    cs = lambda a: pl.BlockSpec(a.shape, lambda b, i: (0,) * a.ndim, pipeline_mode=pl.Buffered(1))
    rows = GQA * tq
    acc = pltpu.VMEM((N_KV_HEADS, rows, LANES), F32)
    return pl.pallas_call(
        functools.partial(_nsa_prompt_body, tq=tq, tk=tk, nwin=nwin),
        grid=(nb, nq),
        in_specs=[qrow(qpad.shape[1]), qrow(LANES), pl.BlockSpec((1,) + ckv.shape[1:], lambda b, i: (b, 0, 0)),
                  seqblk(2 * LANES), seqblk(2 * LANES), seqblk(2 * LANES), seqblk(2 * LANES), cs(ov), cs(go)],
        out_specs=qrow(ATTN_W),
        out_shape=jax.ShapeDtypeStruct((nb * seq, ATTN_W), BF16),
        scratch_shapes=[pltpu.VMEM((N_KV_HEADS, seq // tk, rows, tk), F32), pltpu.VMEM((N_KV_HEADS, rows, LANES), BF16),
                        acc, pltpu.VMEM((N_KV_HEADS, rows, tk), F32), acc,
                        pltpu.SMEM((seq // tk,), jnp.int32)],
        compiler_params=_params(2), name="nsa_prompt",
    )(qpad, gate, ckv, ksa, vsd, kwp, vwd, ov, go)


T8 = 8


def _nsa_sample_body(pt_ref, cache_ref, q_ref, gate_ref, ckv_ref, win_ref, new_ref, e_ref, ov_ref, go_ref,
                     o_ref, buf_ref, sem_ref, *, n_pages, past, n_new, n_cmp):
    def make_copies(seq_idx, slot):
        return [pltpu.make_async_copy(cache_ref.at[pt_ref[seq_idx, j]],
                                      buf_ref.at[slot, :, :, pl.ds(j * PAGE_SIZE, PAGE_SIZE)], sem_ref.at[slot])
                for j in range(n_pages)]

    slot = _gather_step(make_copies)
    rows = N_KV_HEADS * GQA * T8
    ridx = lax.broadcasted_iota(jnp.int32, (rows, 1), 0)
    t_new = ridx & (n_new - 1)
    slope = _alibi_slopes(ridx >> _log2(T8))
    q = q_ref[0]
    new = new_ref[0]

    def tail(x):
        return jnp.concatenate([x, jnp.zeros((LANES - T8, x.shape[1]), F32)], axis=0).astype(BF16)

    def attend(kt, vt, knew, vnew, extra, n_past):
        rel = lax.broadcasted_iota(jnp.int32, (1, n_past + LANES), 1) - n_past
        s = jnp.concatenate([_dot(q, kt) if extra is None else _dot(*extra), _dot_nt(q, knew)], axis=1)
        s = s + slope * rel.astype(F32)
        p = _masked_softmax(s, (rel <= t_new) & (rel >= t_new - WINDOW) & (rel + past >= 0)
                            if extra is None else rel <= t_new)
        return _dot_nt(p[:, :n_past].astype(BF16), vt) + _dot(p[:, n_past:].astype(BF16), vnew)

    nh = ckv_ref.shape[1]
    n_i = lax.broadcasted_iota(jnp.int32, (1, nh), 1)
    dist = (past + t_new) - (n_i * CMP_STRIDE + (CMP_LEN - 1))
    ckv = ckv_ref[0]
    s = _dot_nt(q, ckv[:, :LANES]) - slope * dist.astype(F32)
    pn = _masked_softmax(s, (dist >= 0) & (n_i < n_cmp))
    o_cmp = _dot(pn.astype(BF16), ckv[:, LANES:])
    psum = jnp.concatenate(
        [sum(pn[(k * GQA + g) * T8:(k * GQA + g + 1) * T8] for g in range(GQA)) for k in range(N_KV_HEADS)], axis=0)
    hi, lw = _split_bf16(psum)
    imp = _dot(hi, ov_ref[...]) + _dot(lw, ov_ref[...])
    n_blk = past // SEL_BLOCK
    lane = lax.broadcasted_iota(jnp.int32, (N_KV_HEADS * T8, LANES), 1)
    forced = (lane == 0) | (lane == n_blk - 1)
    score = jnp.where(forced, BIG, jnp.where(lane < n_blk, imp, -BIG))
    cnt = jnp.zeros(score.shape, F32)
    for jp in range(n_blk):
        r = score[:, jp:jp + 1]
        cnt = cnt + _ones_where(r > score) + jnp.where(r == score, _ones_where(lane > jp), 0.0)
    bias = jnp.where(cnt < float(N_SEL - 1), 0.0, NEG).astype(BF16)
    selbias = jnp.concatenate([bias[k * T8:(k + 1) * T8] for k in range(N_KV_HEADS) for _ in range(GQA)], axis=0)
    kt = buf_ref[slot, 0].astype(BF16)
    vt = buf_ref[slot, 1].astype(BF16)
    q_aug = jnp.concatenate([q, selbias], axis=1)
    k_aug = jnp.concatenate([kt, e_ref[...]], axis=0)
    o_slc = attend(kt, vt, tail(new[:, 2 * KV_W:3 * KV_W]), tail(new[:, 3 * KV_W:4 * KV_W]), (q_aug, k_aug), past)
    nw = win_ref.shape[3]
    o_win = attend(win_ref[0, 0].astype(BF16), win_ref[0, 1].astype(BF16),
                   tail(new[:, 4 * KV_W:5 * KV_W]), tail(new[:, 5 * KV_W:6 * KV_W]), None, nw)
    gates = gate_ref[0]
    o = gates[:, 0:1] * o_cmp + gates[:, 1:2] * o_slc + gates[:, 2:3] * o_win
    o_r = pltpu.roll(o, HALF, axis=1)
    lo = lax.broadcasted_iota(jnp.int32, (T8, LANES), 1) < HALF
    groups = []
    for k in range(N_KV_HEADS):
        for i in range(GQA // 2):
            r_even = slice((k * GQA + 2 * i) * T8, (k * GQA + 2 * i + 1) * T8)
            r_odd = slice((k * GQA + 2 * i + 1) * T8, (k * GQA + 2 * i + 2) * T8)
            groups.append(jnp.where(lo, o[r_even], o_r[r_odd]) if k == 0 else jnp.where(lo, o_r[r_even], o[r_odd]))
    o_ref[0] = _rms_norm(jnp.concatenate(groups, axis=1), go_ref[...]).astype(BF16)


def _nsa_sample(page_table, cache_t, q_bd, gate_rows, ckv, win_t, new8, e_mat, ov, go, past, n_new):
    nseq, n_pages = page_table.shape
    assert past == n_pages * PAGE_SIZE and past % SEL_BLOCK == 0 and past // SEL_BLOCK <= LANES
    assert n_new <= T8 and T8 % n_new == 0 and past % CMP_STRIDE == 0
    per_seq = lambda a: pl.BlockSpec((1,) + a.shape[1:], lambda i, pt: (i,) + (0,) * (a.ndim - 1))
    cs = lambda a: pl.BlockSpec(a.shape, lambda i, pt: (0,) * a.ndim, pipeline_mode=pl.Buffered(1))
    return pl.pallas_call(
        functools.partial(_nsa_sample_body, n_pages=n_pages, past=past, n_new=n_new,
                          n_cmp=past // CMP_STRIDE - 1),
        grid_spec=pltpu.PrefetchScalarGridSpec(
            num_scalar_prefetch=1, grid=(nseq,),
            in_specs=[pl.BlockSpec(memory_space=pl.ANY), per_seq(q_bd), per_seq(gate_rows), per_seq(ckv),
                      per_seq(win_t), per_seq(new8), cs(e_mat), cs(ov), cs(go)],
            out_specs=pl.BlockSpec((1, T8, ATTN_W), lambda i, pt: (i, 0, 0)),
            scratch_shapes=[pltpu.VMEM((2, 2, KV_W, past), F32), pltpu.SemaphoreType.DMA((2,))]),
        out_shape=jax.ShapeDtypeStruct((nseq, T8, ATTN_W), BF16),
        compiler_params=_params(1), name="nsa_sample",
    )(page_table, cache_t, q_bd, gate_rows, ckv, win_t, new8, e_mat, ov, go)


def _overlap(n_rows, n_cmp, lane0):
    n = np.arange(n_rows)[:, None]
    j = np.arange(LANES)[None, :] - lane0
    c_end = n * CMP_STRIDE + (CMP_LEN - 1)
    ov = (j * SEL_BLOCK <= c_end) & (j * SEL_BLOCK + SEL_BLOCK > c_end + 1 - CMP_LEN) & (n < n_cmp) & (j >= 0)
    return jnp.asarray(ov, BF16)


def _compress_weights(k_pos, k_w1, k_b1, k_w2, v_pos, v_w1, v_b1, v_w2):
    eye = jnp.eye(4, dtype=F32)
    feat = 4 * HEAD_DIM
    w1 = jnp.stack([k_w1, k_w1, v_w1, v_w1])
    half = lambda a: jnp.einsum("cldh,ce->lcdeh", a, eye).reshape(CMP_STRIDE, feat, 4 * CMP_HIDDEN)
    wab = jnp.concatenate([half(w1[:, :CMP_STRIDE]), half(w1[:, CMP_STRIDE:])], axis=2).astype(BF16)
    pos = jnp.stack([k_pos, k_pos, v_pos, v_pos])
    flat = lambda a: jnp.broadcast_to(jnp.transpose(a, (1, 0, 2)).reshape(CMP_STRIDE, 1, feat), (CMP_STRIDE, 8, feat))
    pos16 = jnp.concatenate([flat(pos[:, :CMP_STRIDE]), flat(pos[:, CMP_STRIDE:])], axis=1).astype(BF16)
    b1 = jnp.concatenate([k_b1, k_b1, v_b1, v_b1]).reshape(1, -1)
    z = jnp.zeros((CMP_HIDDEN, HEAD_DIM), F32)
    w2p = jnp.concatenate([
        jnp.concatenate([k_w2, z, z, z, z, z, z, z], axis=1), jnp.concatenate([z, z, k_w2, z, z, z, z, z], axis=1),
        jnp.concatenate([z, z, z, z, v_w2, z, z, z], axis=1), jnp.concatenate([z, z, z, z, z, z, v_w2, z], axis=1),
    ], axis=0).astype(BF16)
    w2s = jnp.concatenate([
        jnp.concatenate([k_w2, z, z, z], axis=1), jnp.concatenate([z, k_w2, z, z], axis=1),
        jnp.concatenate([z, z, v_w2, z], axis=1), jnp.concatenate([z, z, z, v_w2], axis=1)], axis=0).astype(BF16)
    return wab, pos16, b1, w2p, w2s


def _proj_weight(w_in):
    d = w_in.shape[0]
    g0 = ATTN_W + 6 * KV_W
    wg = jnp.concatenate([w_in[:, g0:g0 + 3 * N_HEADS], jnp.zeros((d, LANES - 3 * N_HEADS), F32)], axis=1)
    return jnp.concatenate([w_in[:, :ATTN_W] * HEAD_DIM ** -0.5, w_in[:, ATTN_W:g0], wg,
                            w_in[:, g0 + 3 * N_HEADS:]], axis=1).astype(BF16)


def _feature_major(a):
    lead = a.shape[:-4]
    n = len(lead)
    a = jnp.transpose(a, tuple(range(n)) + (n + 1, n + 2, n + 3, n))
    return a.reshape(lead + (2, KV_W, a.shape[-1]))


def _token_major(a, nb):
    return jnp.transpose(a.reshape(nb, 2, N_KV_HEADS, HEAD_DIM, a.shape[-1]), (0, 4, 1, 2, 3))


def kernel(x_prompt, x_sample, cache_cmp_kv, cache_slc_kv, state_win_kv, page_table, ln1_g, ln1_b, ffn1_w_in, ffn1_w_out, w_in, cmp_k_pos, cmp_k_w1, cmp_k_b1, cmp_k_w2, cmp_v_pos, cmp_v_w1, cmp_v_b1, cmp_v_w2, sgu_ln_g, sgu_ln_b, sgu_w, sgu_b, out_norm_g, w_out, ln2_g, ln2_b, ffn2_w_in, ffn2_w_out, ln3_g, ln3_b):
    depth = w_in.shape[0]
    nb, seq, d = x_prompt.shape
    ns, n_new, _ = x_sample.shape
    n_pages = page_table.shape[1]
    past = n_pages * PAGE_SIZE
    d_ff = ffn1_w_out.shape[1]
    alpha = (2.0 * depth) ** 0.25
    row = lambda a: a.reshape(1, -1)
    hp = x_prompt.reshape(nb * seq, d)
    hs = x_sample.reshape(ns * n_new, d)
    kv5 = (N_KV_HEADS, HEAD_DIM)
    outs = [[] for _ in range(7)]
    ov_p = _overlap(seq // CMP_STRIDE, seq // CMP_STRIDE - 1, HALF)
    ov_s = _overlap(past // CMP_STRIDE, past // CMP_STRIDE - 1, 0)
    e_mat = jnp.asarray(np.arange(LANES)[:, None] == np.arange(past)[None, :] // SEL_BLOCK, BF16)
    for l in range(depth):
        f1 = (ffn1_w_in[l, :, :d_ff].astype(BF16), ffn1_w_in[l, :, d_ff:].astype(BF16), ffn1_w_out[l].astype(BF16),
              row(ln1_g[l]), row(ln1_b[l]))
        f2 = (ffn2_w_in[l, :, :d_ff].astype(BF16), ffn2_w_in[l, :, d_ff:].astype(BF16), ffn2_w_out[l].astype(BF16),
              row(ln3_g[l]), row(ln3_b[l]))
        wp = _proj_weight(w_in[l])
        wab, pos16, b1, w2p, w2s = _compress_weights(cmp_k_pos[l], cmp_k_w1[l], cmp_k_b1[l], cmp_k_w2[l],
                                                     cmp_v_pos[l], cmp_v_w1[l], cmp_v_b1[l], cmp_v_w2[l])
        sg, sb = row(sgu_ln_g[l]), row(sgu_ln_b[l])
        go_a, go_g = row(out_norm_g[l, :ATTN_W]), row(out_norm_g[l, ATTN_W:])
        wm = w_out[l].astype(BF16)
        mix_p = jnp.tril(sgu_w[l]).astype(BF16)
        mb_p = jnp.repeat(sgu_b[l].T, GMLP_DIM, axis=1)
        r = min(n_new, CHUNK)
        mix_s = jax.vmap(lambda w: jnp.kron(jnp.eye(CHUNK // r, dtype=F32), w))(jnp.tril(sgu_w[l, :, :r, :r])).astype(BF16)
        mb_s = jnp.tile(jnp.repeat(sgu_b[l, :, :r].T, GMLP_DIM, axis=1), (CHUNK // r, 1))

        hp = _ffn_ln(hp, *f1, alpha)
        (qpad, cmpk_p, cmpv_p, _, _, cmp_t, slc_t, win_t, ksa, vsd, kwp, vwd, gate, ogn, _) = _proj(
            hp, wp, sg, sb, mix_p, mb_p, go_g, seq)
        ckv = _compress_prompt(cmpk_p, cmpv_p, nb, wab, pos16, b1, w2p)
        oan = _nsa_prompt(qpad, gate, ckv, ksa, vsd, kwp, vwd, ov_p, go_a, nb, seq)
        hp = _ffn_ln(hp, *f2, alpha, merge_args=(oan, ogn, wm, row(ln2_g[l]), row(ln2_b[l])))
        outs[0].append(_token_major(cmp_t, nb))
        outs[2].append(_token_major(slc_t, nb))
        outs[4].append(_token_major(win_t[:, :, seq - min(WINDOW, seq):], nb))

        hs = _ffn_ln(hs, *f1, alpha)
        (qpad, cmpk_s, cmpv_s, slc_s, win_s, _, _, _, _, _, _, _, gate, ogn, v_s) = _proj(
            hs, wp, sg, sb, mix_s, mb_s, go_g, ns * n_new)
        cmp_s = jnp.concatenate([cmpk_s, cmpv_s], axis=1)
        ckv = _compress_sample(page_table, _feature_major(cache_cmp_kv[l]), wab, pos16, b1, w2s)
        q5 = qpad.reshape(ns, n_new, N_KV_HEADS, GQA, LANES)[..., :HEAD_DIM]
        q5 = jnp.pad(jnp.transpose(q5, (0, 2, 3, 1, 4)), ((0, 0), (0, 0), (0, 0), (0, T8 - n_new), (0, 0)))
        zq = jnp.zeros_like(q5[:, 0])
        q_bd = jnp.stack([jnp.concatenate([q5[:, 0], zq], axis=-1), jnp.concatenate([zq, q5[:, 1]], axis=-1)],
                         axis=1).reshape(ns, N_KV_HEADS * GQA * T8, LANES)
        g5 = gate[:, :3 * N_HEADS].reshape(ns, n_new, 3, N_KV_HEADS, GQA)
        g5 = jnp.pad(jnp.transpose(g5, (0, 3, 4, 1, 2)), ((0, 0), (0, 0), (0, 0), (0, T8 - n_new), (0, 0)))
        gate_rows = g5.reshape(ns, N_KV_HEADS * GQA * T8, 3)
        new8 = jnp.pad(jnp.concatenate([cmp_s, slc_s, win_s], axis=1).reshape(ns, n_new, 6 * KV_W),
                       ((0, 0), (0, T8 - n_new), (0, 0)))
        oan = _nsa_sample(page_table, _feature_major(cache_slc_kv[l]), q_bd, gate_rows, ckv,
                          _feature_major(state_win_kv[l]), new8, e_mat, ov_s, go_a, past, n_new)
        oan = oan[:, :n_new].reshape(ns * n_new, ATTN_W)
        hs = _ffn_ln(hs, *f2, alpha, merge_args=(oan, ogn, wm, row(ln2_g[l]), row(ln2_b[l])))
        win_all = jnp.concatenate([state_win_kv[l], win_s.reshape(ns, n_new, 2, *kv5)], axis=1)
        outs[1].append(cmp_s.reshape(ns, n_new, 2, *kv5))
        outs[3].append(slc_s.reshape(ns, n_new, 2, *kv5))
        outs[5].append(win_all[:, -min(WINDOW, past + n_new):])
        outs[6].append(v_s.reshape(ns, n_new, N_GMLP, GMLP_DIM))
    st = [jnp.stack(o, axis=0) for o in outs]
    return (hp.reshape(nb, seq, d), hs.reshape(ns, n_new, d), st[0], st[1], st[2], st[3], st[4], st[5], st[6])
```

```python
import functools

import numpy as np
import jax
import jax.numpy as jnp
from jax import lax
from jax.experimental import pallas as pl
from jax.experimental.pallas import tpu as pltpu

HEAD_DIM = 64
N_HEADS = 8
N_KV_HEADS = 2
GQA = N_HEADS // N_KV_HEADS
ATTN_W = N_HEADS * HEAD_DIM
KV_W = N_KV_HEADS * HEAD_DIM
CMP_LEN = 32
CMP_STRIDE = 16
CMP_HIDDEN = 64
SEL_BLOCK = 64
N_SEL = 16
WINDOW = 512
N_GMLP = 8
GMLP_DIM = 64
GMLP_W = N_GMLP * GMLP_DIM
CHUNK = 128
PAGE_SIZE = 128
LN_EPS = 1e-5
NEG = -1e30
BIG = 1e9

LANES = 128
HALF = LANES // 2
VMEM_LIMIT = 56 * 1024 * 1024
SLOPES = [2.0 ** -(h + 1) for h in range(N_HEADS)]

F32 = jnp.float32
BF16 = jnp.bfloat16

_Q0, _KV0, _G0, _U0, _V0, _PW = 0, ATTN_W, ATTN_W + 6 * KV_W, ATTN_W + 6 * KV_W + LANES, \
    ATTN_W + 6 * KV_W + LANES + GMLP_W, ATTN_W + 6 * KV_W + LANES + 2 * GMLP_W


def _dot(a, b):
    return jnp.dot(a, b, preferred_element_type=F32)


def _dot_nt(a, b):
    return lax.dot_general(a, b, (((1,), (1,)), ((), ())), preferred_element_type=F32)


def _layer_norm(y, g, b):
    mu = jnp.mean(y, axis=-1, keepdims=True)
    d = y - mu
    var = jnp.mean(d * d, axis=-1, keepdims=True)
    return d * lax.rsqrt(var + LN_EPS) * g + b


def _rms_norm(y, g):
    return y * lax.rsqrt(jnp.mean(y * y, axis=-1, keepdims=True) + LN_EPS) * g


def _ones_where(mask):
    return jnp.where(mask, 1.0, 0.0)


def _alibi_lanes(lane, hi, lo):
    return jnp.where(lane == HALF, hi, jnp.where(lane == HALF + 1, lo, 0.0))


def _log2(n):
    assert n > 0 and n & (n - 1) == 0, n
    return n.bit_length() - 1


def _const_spec(shape):
    nd = len(shape)
    return pl.BlockSpec(shape, lambda *_: (0,) * nd, pipeline_mode=pl.Buffered(1))


def _params(n_grid):
    return pltpu.CompilerParams(dimension_semantics=("arbitrary",) * n_grid, vmem_limit_bytes=VMEM_LIMIT)


def _ffn_body(*refs, alpha, merge):
    if merge:
        (h_ref, oa_ref, og_ref, wm_ref, gm_ref, bm_ref,
         wa_ref, wb_ref, wo_ref, g_ref, b_ref, o_ref) = refs
        cat = jnp.concatenate([oa_ref[...], og_ref[...]], axis=1)
        x = _layer_norm(alpha * h_ref[...] + _dot(cat, wm_ref[...]), gm_ref[...], bm_ref[...])
    else:
        x_ref, wa_ref, wb_ref, wo_ref, g_ref, b_ref, o_ref = refs
        x = x_ref[...]
    xb = x.astype(BF16)
    a = _dot(xb, wa_ref[...])
    b = _dot(xb, wb_ref[...])
    hid = ((a * jax.nn.sigmoid(a)) * b).astype(BF16)
    y = alpha * x + 0.5 * _dot(hid, wo_ref[...])
    o_ref[...] = _layer_norm(y, g_ref[...], b_ref[...])


def _ffn_ln(x, wa, wb, wo, g, b, alpha, merge_args=None, tm=512):
    t, d = x.shape
    tm = min(tm, t)
    assert t % tm == 0
    row = lambda w: pl.BlockSpec((tm, w), lambda i: (i, 0))
    ins, specs = [x], [row(d)]
    if merge_args is not None:
        oa, og, wm, gm, bm = merge_args
        ins += [oa, og, wm, gm, bm]
        specs += [row(oa.shape[1]), row(og.shape[1]), _const_spec(wm.shape), _const_spec(gm.shape),
                  _const_spec(bm.shape)]
    ins += [wa, wb, wo, g, b]
    specs += [_const_spec(a.shape) for a in (wa, wb, wo, g, b)]
    return pl.pallas_call(
        functools.partial(_ffn_body, alpha=alpha, merge=merge_args is not None),
        grid=(t // tm,), in_specs=specs, out_specs=row(d),
        out_shape=jax.ShapeDtypeStruct((t, d), F32), compiler_params=_params(1),
        name="ffn_merge_ln" if merge_args is not None else "ffn_ln",
    )(*ins)


def _proj_body(h_ref, w_ref, sg_ref, sb_ref, mix_ref, mb_ref, go_ref,
               qpad_ref, cmpk_ref, cmpv_ref, slc_ref, win_ref, cmpt_ref, slct_ref, wint_ref,
               ksa_ref, vsd_ref, kwp_ref, vwd_ref, gate_ref, ogn_ref, v_ref, *, tm, seq):
    p = _dot(h_ref[...].astype(BF16), w_ref[...])
    lane = lax.broadcasted_iota(jnp.int32, (tm, LANES), 1)
    lo = lane < HALF
    tok = lax.rem(pl.program_id(0) * tm, seq) + lax.broadcasted_iota(jnp.int32, (tm, LANES), 0)
    for i in range(ATTN_W // LANES):
        x = p[:, _Q0 + i * LANES:_Q0 + (i + 1) * LANES]
        xr = pltpu.roll(x, HALF, axis=1)
        qpad_ref[:, (2 * i) * LANES:(2 * i + 1) * LANES] = jnp.where(
            lo, x, _alibi_lanes(lane, LANES * SLOPES[2 * i], SLOPES[2 * i])).astype(BF16)
        qpad_ref[:, (2 * i + 1) * LANES:(2 * i + 2) * LANES] = jnp.where(
            lo, xr, _alibi_lanes(lane, LANES * SLOPES[2 * i + 1], SLOPES[2 * i + 1])).astype(BF16)
    for s, feat_ref in enumerate((cmpt_ref, slct_ref, wint_ref)):
        feat_ref[0] = p[:, _KV0 + 2 * s * KV_W:_KV0 + 2 * (s + 1) * KV_W].T
    cmpk_ref[...] = p[:, _KV0:_KV0 + KV_W]
    cmpv_ref[...] = p[:, _KV0 + KV_W:_KV0 + 2 * KV_W]
    slc_ref[...] = p[:, _KV0 + 2 * KV_W:_KV0 + 4 * KV_W]
    win_ref[...] = p[:, _KV0 + 4 * KV_W:_KV0 + 6 * KV_W]
    onehot = _ones_where(lane - HALF == (tok >> _log2(SEL_BLOCK)))
    ks = p[:, _KV0 + 2 * KV_W:_KV0 + 3 * KV_W]
    ksr = pltpu.roll(ks, HALF, axis=1)
    ksa_ref[:, :LANES] = jnp.where(lo, ks, onehot).astype(BF16)
    ksa_ref[:, LANES:] = jnp.where(lo, ksr, onehot).astype(BF16)
    vs = p[:, _KV0 + 3 * KV_W:_KV0 + 4 * KV_W]
    vsr = pltpu.roll(vs, HALF, axis=1)
    vsd_ref[:, :LANES] = jnp.where(lo, vs, 1.0).astype(BF16)
    vsd_ref[:, LANES:] = jnp.where(lo, vsr, 1.0).astype(BF16)
    kw = p[:, _KV0 + 4 * KV_W:_KV0 + 5 * KV_W]
    kwr = pltpu.roll(kw, HALF, axis=1)
    pos_pair = _alibi_lanes(lane, (tok >> _log2(LANES)).astype(F32), (tok & (LANES - 1)).astype(F32))
    kwp_ref[:, :LANES] = jnp.where(lo, kw, pos_pair).astype(BF16)
    kwp_ref[:, LANES:] = jnp.where(lo, kwr, pos_pair).astype(BF16)
    vw = p[:, _KV0 + 5 * KV_W:_KV0 + 6 * KV_W]
    vwr = pltpu.roll(vw, HALF, axis=1)
    vwd_ref[:, :LANES] = jnp.where(lo, vw, 1.0).astype(BF16)
    vwd_ref[:, LANES:] = jnp.where(lo, vwr, 1.0).astype(BF16)
    gate_ref[...] = jax.nn.sigmoid(p[:, _G0:_G0 + LANES])
    u = jax.nn.gelu(p[:, _U0:_U0 + GMLP_W])
    v = _layer_norm(jax.nn.gelu(p[:, _V0:_V0 + GMLP_W]), sg_ref[...], sb_ref[...])
    v_ref[...] = v
    vb = v.astype(BF16)
    lo_c = lax.broadcasted_iota(jnp.int32, (CHUNK, LANES), 1) < HALF
    for c in range(tm // CHUNK):
        rows = slice(c * CHUNK, (c + 1) * CHUNK)
        zs = []
        for i in range(GMLP_W // LANES):
            vp = vb[rows, i * LANES:(i + 1) * LANES]
            zs.append(jnp.where(lo_c, _dot(mix_ref[2 * i], vp), _dot(mix_ref[2 * i + 1], vp)))
        z = jnp.concatenate(zs, axis=1) + mb_ref[...]
        ogn_ref[rows, :] = _rms_norm(u[rows, :] * z, go_ref[...]).astype(BF16)


def _proj(h, w, sg, sb, mix, mb, go, seq, tm=512):
    t, d = h.shape
    tm = min(tm, t)
    assert t % tm == 0 and tm % CHUNK == 0 and seq % tm == 0 and t % seq == 0 and seq // SEL_BLOCK <= HALF
    nq = seq // tm
    row = lambda width: pl.BlockSpec((tm, width), lambda i: (i, 0))
    feat = pl.BlockSpec((1, 2 * KV_W, tm), lambda i: (i // nq, 0, i % nq))
    tok_major = [(2 * ATTN_W, BF16), (KV_W, F32), (KV_W, F32), (2 * KV_W, F32), (2 * KV_W, F32)]
    rest = [(2 * LANES, BF16), (2 * LANES, BF16), (2 * LANES, BF16), (2 * LANES, BF16), (LANES, F32),
            (GMLP_W, BF16), (GMLP_W, F32)]
    sds = jax.ShapeDtypeStruct
    return pl.pallas_call(
        functools.partial(_proj_body, tm=tm, seq=seq),
        grid=(t // tm,),
        in_specs=[row(d)] + [_const_spec(a.shape) for a in (w, sg, sb, mix, mb, go)],
        out_specs=[row(wd) for wd, _ in tok_major] + [feat] * 3 + [row(wd) for wd, _ in rest],
        out_shape=[sds((t, wd), dt) for wd, dt in tok_major] + [sds((t // seq, 2 * KV_W, seq), F32)] * 3
        + [sds((t, wd), dt) for wd, dt in rest],
        compiler_params=_params(1), name="proj",
    )(h, w, sg, sb, mix, mb, go)


def _compress_math(rows_at, nh, wab_ref, pos_ref, b1_ref, w2_ref):
    hw = b1_ref.shape[1]
    ab = None
    for l in range(CMP_STRIDE):
        lhs = jnp.concatenate([rows_at(l).astype(BF16), pos_ref[l]], axis=0)
        d = _dot(lhs, wab_ref[l])
        ab = d if ab is None else ab + d
    c = ab[nh:nh + 1, :hw] + ab[nh + 8:nh + 9, hw:] + b1_ref[...]
    hid = ab[:nh, :hw] + pltpu.roll(ab[:nh, hw:], nh - 1, axis=0) + c
    return _dot(jax.nn.gelu(hid).astype(BF16), w2_ref[...])


def _strided_rows(k_ref, v_ref, nh):
    return lambda l: jnp.concatenate([k_ref[pl.ds(l, nh, stride=CMP_STRIDE), :],
                                      v_ref[pl.ds(l, nh, stride=CMP_STRIDE), :]], axis=1)


def _compress_body(xk_ref, xv_ref, wab_ref, pos_ref, b1_ref, w2_ref, o_ref):
    nh = o_ref.shape[1]
    out = _compress_math(_strided_rows(xk_ref, xv_ref, nh), nh, wab_ref, pos_ref, b1_ref, w2_ref)
    lane = lax.broadcasted_iota(jnp.int32, (nh, LANES), 1)
    c_end = lax.broadcasted_iota(jnp.int32, (nh, LANES), 0) * CMP_STRIDE + (CMP_LEN - 1)
    pair = _alibi_lanes(lane, (c_end >> _log2(LANES)).astype(F32), (c_end & (LANES - 1)).astype(F32))
    for k in range(N_KV_HEADS):
        o_ref[0, :, k * LANES:(k + 1) * LANES] = (out[:, k * LANES:(k + 1) * LANES] + pair).astype(o_ref.dtype)
    o_ref[0, :, N_KV_HEADS * LANES:] = out[:, N_KV_HEADS * LANES:].astype(o_ref.dtype)


def _compress_prompt(xk, xv, nb, wab, pos, b1, w2):
    seq = xk.shape[0] // nb
    nh = seq // CMP_STRIDE
    return pl.pallas_call(
        _compress_body, grid=(nb,),
        in_specs=[pl.BlockSpec((seq, KV_W), lambda i: (i, 0))] * 2 + [_const_spec(a.shape) for a in (wab, pos, b1, w2)],
        out_specs=pl.BlockSpec((1, nh, w2.shape[1]), lambda i: (i, 0, 0)),
        out_shape=jax.ShapeDtypeStruct((nb, nh, w2.shape[1]), BF16),
        compiler_params=_params(1), name="compress_prompt",
    )(xk, xv, wab, pos, b1, w2)


def _gather_step(make_copies):
    i, n = pl.program_id(0), pl.num_programs(0)
    slot = lax.rem(i, 2)

    @pl.when(i == 0)
    def _():
        for cp in make_copies(i, slot):
            cp.start()

    @pl.when(i + 1 < n)
    def _():
        for cp in make_copies(i + 1, 1 - slot):
            cp.start()

    for cp in make_copies(i, slot):
        cp.wait()
    return slot


def _compress_sample_body(pt_ref, cache_ref, perm_ref, wab_ref, pos_ref, b1_ref, w2_ref, o_ref, buf_ref, xs_ref,
                          sem_ref, *, n_pages):
    def make_copies(seq_idx, slot):
        return [pltpu.make_async_copy(cache_ref.at[pt_ref[seq_idx, j]], buf_ref.at[slot, j], sem_ref.at[slot])
                for j in range(n_pages)]

    slot = _gather_step(make_copies)
    per_page = PAGE_SIZE // CMP_STRIDE

    feat, page = buf_ref.shape[3], buf_ref.shape[4]
    xt = buf_ref[slot].reshape(n_pages * 2 * feat, page)
    z = _dot_nt(perm_ref[...], xt.astype(BF16))
    for j in range(n_pages):
        for c in range(2):
            col = slice((2 * j + c) * feat, (2 * j + c + 1) * feat)
            for l in range(CMP_STRIDE):
                xs_ref[c, l, j * per_page:(j + 1) * per_page, :] = z[l * per_page:(l + 1) * per_page, col]
    nh = o_ref.shape[1]
    rows_at = lambda l: jnp.concatenate([xs_ref[0, l], xs_ref[1, l]], axis=1)
    o_ref[0] = _compress_math(rows_at, nh, wab_ref, pos_ref, b1_ref, w2_ref).astype(o_ref.dtype)


def _compress_sample(page_table, cache_t, wab, pos, b1, w2):
    nseq, n_pages = page_table.shape
    _, _, feat, page = cache_t.shape
    nh = n_pages * page // CMP_STRIDE
    cs = lambda a: pl.BlockSpec(a.shape, lambda i, pt: (0,) * a.ndim, pipeline_mode=pl.Buffered(1))
    tok = np.arange(page)
    perm = jnp.asarray((tok % CMP_STRIDE * (page // CMP_STRIDE) + tok // CMP_STRIDE)[None, :] == tok[:, None], BF16)
    return pl.pallas_call(
        functools.partial(_compress_sample_body, n_pages=n_pages),
        grid_spec=pltpu.PrefetchScalarGridSpec(
            num_scalar_prefetch=1, grid=(nseq,),
            in_specs=[pl.BlockSpec(memory_space=pl.ANY)] + [cs(a) for a in (perm, wab, pos, b1, w2)],
            out_specs=pl.BlockSpec((1, nh, w2.shape[1]), lambda i, pt: (i, 0, 0)),
            scratch_shapes=[pltpu.VMEM((2, n_pages, 2, feat, page), F32), pltpu.VMEM((2, CMP_STRIDE, nh, feat), F32),
                            pltpu.SemaphoreType.DMA((2,))]),
        out_shape=jax.ShapeDtypeStruct((nseq, nh, w2.shape[1]), BF16),
        compiler_params=_params(1), name="compress_sample",
    )(page_table, cache_t, perm, wab, pos, b1, w2)


def _masked_softmax(s, mask):
    sm = jnp.where(mask, s, NEG)
    m = jnp.max(sm, axis=-1, keepdims=True)
    p = jnp.where(mask, jnp.exp(sm - m), 0.0)
    l = jnp.sum(p, axis=-1, keepdims=True)
    return p * (1.0 / jnp.maximum(l, 1e-30))


def _split_bf16(x):
    hi = x.astype(BF16)
    return hi, (x - hi.astype(F32)).astype(BF16)


def _alibi_slopes(head_idx):
    out = jnp.zeros(head_idx.shape, F32)
    for h in range(N_HEADS):
        out = jnp.where(head_idx == h, SLOPES[h], out)
    return out


def _nsa_prompt_body(q_ref, gate_ref, ckv_ref, ksa_ref, vsd_ref, kwp_ref, vwd_ref, ov_ref, go_ref,
                     o_ref, s_ref, qsel_ref, mrun_ref, mb_ref, acc_ref, tile_ref, *, tq, tk, nwin):
    n_tiles = s_ref.shape[1]
    t0 = pl.program_id(1) * tq
    rows = GQA * tq
    lane = lax.broadcasted_iota(jnp.int32, (tq, LANES), 1)
    lo = lane < HALF
    t_rel = lax.broadcasted_iota(jnp.int32, (rows, 1), 0) & (tq - 1)
    cur = (t0 + lax.broadcasted_iota(jnp.int32, (tq, 1), 0)) >> _log2(SEL_BLOCK)
    sub = lax.broadcasted_iota(jnp.int32, (8, tq), 0)
    gates = gate_ref[...]
    nh = ckv_ref.shape[1]
    c_rel = lax.broadcasted_iota(jnp.int32, (1, nh), 1) * CMP_STRIDE + (CMP_LEN - 1) - t0
    n_full = t0 // tk
    w_start = pl.multiple_of(jnp.maximum(t0 - WINDOW, 0), tq)
    w_rel = (w_start - t0) + lax.broadcasted_iota(jnp.int32, (1, nwin), 1)
    t_q = lax.broadcasted_iota(jnp.int32, (tq, 1), 0)
    w_bias = jnp.where((w_rel <= t_q) & (w_rel >= t_q - WINDOW), 0.0, NEG)
    cols = [slice(k * LANES, (k + 1) * LANES) for k in range(N_KV_HEADS)]
    blk_tile = lax.broadcasted_iota(jnp.int32, (HALF, 1), 0) >> _log2(tk // SEL_BLOCK)

    def add_alibi(s, kvh, rel):
        relf = rel.astype(F32)
        return jnp.concatenate([s[g * tq:(g + 1) * tq] + SLOPES[kvh * GQA + g] * relf for g in range(GQA)], axis=0)

    o_cmp, o_win = [], []
    tile_hits = [jnp.int32(0)] * n_tiles
    for kvh in range(N_KV_HEADS):
        qs = [q_ref[:, (kvh * GQA + g) * LANES:(kvh * GQA + g + 1) * LANES] for g in range(GQA)]
        qall = jnp.concatenate(qs, axis=0)
        pn = _masked_softmax(_dot_nt(qall, ckv_ref[0, :, cols[kvh]]), c_rel <= t_rel)
        o_cmp.append(_dot(pn.astype(BF16), ckv_ref[0, :, 2 * LANES + kvh * LANES:2 * LANES + (kvh + 1) * LANES]))
        psum = pn[0:tq] + pn[tq:2 * tq] + pn[2 * tq:3 * tq] + pn[3 * tq:4 * tq]
        hi, lw = _split_bf16(psum)
        imp = _dot(hi, ov_ref[...]) + _dot(lw, ov_ref[...])
        j = lane - HALF
        forced = (j == 0) | (j == cur) | (j == cur - 1)
        score = jnp.where(forced, BIG, jnp.where(j <= cur, imp, -BIG))
        sc_t = score.T[HALF:, :]
        cnt = [jnp.zeros((8, tq), F32) for _ in range(HALF // 8)]
        for jp in range(HALF):
            r = sc_t[jp:jp + 1, :]
            for v in range(HALF // 8):
                blk = sc_t[8 * v:8 * v + 8, :]
                if 8 * v > jp:
                    beats = _ones_where(r >= blk)
                elif 8 * v + 7 <= jp:
                    beats = _ones_where(r > blk)
                else:
                    beats = _ones_where(r > blk) + jnp.where(r == blk, _ones_where(sub > jp - 8 * v), 0.0)
                cnt[v] = cnt[v] + beats
        sel = jnp.concatenate(cnt, axis=0) < float(N_SEL)
        selbias = jnp.concatenate([jnp.zeros((HALF, tq), F32), jnp.where(sel, 0.0, NEG)], axis=0).T.astype(BF16)
        qsel_ref[kvh] = jnp.concatenate([jnp.where(lo, q, selbias) for q in qs], axis=0)
        any_sel = jnp.max(_ones_where(sel), axis=-1, keepdims=True)
        for w in range(-(-n_tiles // 8)):
            weight = jnp.zeros((HALF, 1), F32)
            for k in range(8):
                weight = jnp.where(blk_tile == 8 * w + k, float(8 ** k), weight)
            packed = jnp.sum(any_sel * weight).astype(jnp.int32)
            for k in range(min(8, n_tiles - 8 * w)):
                tile_hits[8 * w + k] = tile_hits[8 * w + k] + ((packed >> (3 * k)) & 7)
        s = _dot_nt(qall, kwp_ref[pl.ds(w_start, nwin), cols[kvh]])
        s = jnp.concatenate([s[g * tq:(g + 1) * tq] + w_bias for g in range(GQA)], axis=0)
        pw = jnp.exp((s - jnp.max(s, axis=-1, keepdims=True)).astype(BF16))
        o_win.append(_dot(pw, vwd_ref[pl.ds(w_start, nwin), cols[kvh]]))
        mrun_ref[kvh] = jnp.full((rows, LANES), NEG, F32)
        acc_ref[kvh] = jnp.zeros((rows, LANES), F32)

    def scores(kvh, jt, causal):
        k0 = pl.multiple_of(jt * tk, tk)
        k_rel = (k0 - t0) + lax.broadcasted_iota(jnp.int32, (1, tk), 1)
        s = add_alibi(_dot_nt(qsel_ref[kvh], ksa_ref[pl.ds(k0, tk), cols[kvh]]), kvh, k_rel)
        if causal:
            s = jnp.where(k_rel <= t_rel, s, NEG)
        s_ref[kvh, jt] = s
        mt = s[:, :LANES]
        for c in range(1, tk // LANES):
            mt = jnp.maximum(mt, s[:, c * LANES:(c + 1) * LANES])
        mrun_ref[kvh] = jnp.maximum(mrun_ref[kvh], mt)

    n_hit = jnp.int32(0)
    for jt in range(n_tiles):
        tile_ref[n_hit] = jt
        n_hit = n_hit + ((tile_hits[jt] > 0) & (jt < n_full)).astype(jnp.int32)

    def sweep(visit):
        def pair(i, carry):
            for u in range(2):
                for kvh in range(N_KV_HEADS):
                    visit(kvh, tile_ref[2 * i + u])
            return carry

        lax.fori_loop(0, n_hit >> 1, pair, 0)

        @pl.when((n_hit & 1) == 1)
        def _():
            for kvh in range(N_KV_HEADS):
                visit(kvh, tile_ref[n_hit - 1])

    sweep(lambda kvh, jt: scores(kvh, jt, False))
    for kvh in range(N_KV_HEADS):
        scores(kvh, n_full, True)
        mb_ref[kvh] = jnp.broadcast_to(jnp.max(mrun_ref[kvh], axis=-1, keepdims=True), (rows, tk))

    def weigh(kvh, jt):
        k0 = pl.multiple_of(jt * tk, tk)
        p = jnp.exp((s_ref[kvh, jt] - mb_ref[kvh]).astype(BF16))
        acc_ref[kvh] = acc_ref[kvh] + _dot(p, vsd_ref[pl.ds(k0, tk), cols[kvh]])

    sweep(weigh)
    for kvh in range(N_KV_HEADS):
        weigh(kvh, n_full)
    head_out = []
    for kvh in range(N_KV_HEADS):
        gate_col = [jnp.concatenate([gates[:, b * N_HEADS + kvh * GQA + g:b * N_HEADS + kvh * GQA + g + 1]
                                     for g in range(GQA)], axis=0) for b in range(3)]
        pv_slc, pv_win = acc_ref[kvh], o_win[kvh]
        out = (gate_col[0] * o_cmp[kvh]
               + (gate_col[1] * (1.0 / pv_slc[:, HALF:HALF + 1])) * pv_slc
               + (gate_col[2] * (1.0 / pv_win[:, HALF:HALF + 1])) * pv_win)
        head_out += [out[g * tq:(g + 1) * tq] for g in range(GQA)]
    o = jnp.concatenate([jnp.where(lo, head_out[2 * i], pltpu.roll(head_out[2 * i + 1], HALF, axis=1))
                         for i in range(N_HEADS // 2)], axis=1)
    o_ref[...] = _rms_norm(o, go_ref[...]).astype(BF16)


def _nsa_prompt(qpad, gate, ckv, ksa, vsd, kwp, vwd, ov, go, nb, seq, tq=256, tk=256):
    assert seq % tk == 0 and tk % tq == 0 and tq % LANES == 0 and seq >= WINDOW + tq and tk // SEL_BLOCK <= 7
    nwin = WINDOW + tq
    nq = seq // tq
    qrow = lambda w: pl.BlockSpec((tq, w), lambda b, i: (b * nq + i, 0))
    seqblk = lambda w: pl.BlockSpec((seq, w), lambda b, i: (b, 0), pipeline_mode=pl.Buffered(1))
    cs = lambda a: pl.BlockSpec(a.shape, lambda b, i: (0,) * a.ndim, pipeline_mode=pl.Buffered(1))
    rows = GQA * tq
    acc = pltpu.VMEM((N_KV_HEADS, rows, LANES), F32)
    return pl.pallas_call(
        functools.partial(_nsa_prompt_body, tq=tq, tk=tk, nwin=nwin),
        grid=(nb, nq),
        in_specs=[qrow(qpad.shape[1]), qrow(LANES), pl.BlockSpec((1,) + ckv.shape[1:], lambda b, i: (b, 0, 0)),
                  seqblk(2 * LANES), seqblk(2 * LANES), seqblk(2 * LANES), seqblk(2 * LANES), cs(ov), cs(go)],
        out_specs=qrow(ATTN_W),
        out_shape=jax.ShapeDtypeStruct((nb * seq, ATTN_W), BF16),
        scratch_shapes=[pltpu.VMEM((N_KV_HEADS, seq // tk, rows, tk), F32), pltpu.VMEM((N_KV_HEADS, rows, LANES), BF16),
                        acc, pltpu.VMEM((N_KV_HEADS, rows, tk), F32), acc,
                        pltpu.SMEM((seq // tk,), jnp.int32)],
        compiler_params=_params(2), name="nsa_prompt",
    )(qpad, gate, ckv, ksa, vsd, kwp, vwd, ov, go)


T8 = 8


def _nsa_sample_body(pt_ref, cache_ref, q_ref, gate_ref, ckv_ref, win_ref, new_ref, e_ref, ov_ref, go_ref,
                     o_ref, buf_ref, sem_ref, *, n_pages, past, n_new, n_cmp):
    def make_copies(seq_idx, slot):
        return [pltpu.make_async_copy(cache_ref.at[pt_ref[seq_idx, j]],
                                      buf_ref.at[slot, :, :, pl.ds(j * PAGE_SIZE, PAGE_SIZE)], sem_ref.at[slot])
                for j in range(n_pages)]

    slot = _gather_step(make_copies)
    rows = N_KV_HEADS * GQA * T8
    ridx = lax.broadcasted_iota(jnp.int32, (rows, 1), 0)
    t_new = ridx & (n_new - 1)
    slope = _alibi_slopes(ridx >> _log2(T8))
    q = q_ref[0]
    new = new_ref[0]

    def tail(x):
        return jnp.concatenate([x, jnp.zeros((LANES - T8, x.shape[1]), F32)], axis=0).astype(BF16)

    def attend(kt, vt, knew, vnew, extra, n_past):
        rel = lax.broadcasted_iota(jnp.int32, (1, n_past + LANES), 1) - n_past
        s = jnp.concatenate([_dot(q, kt) if extra is None else _dot(*extra), _dot_nt(q, knew)], axis=1)
        s = s + slope * rel.astype(F32)
        p = _masked_softmax(s, (rel <= t_new) & (rel >= t_new - WINDOW) & (rel + past >= 0)
                            if extra is None else rel <= t_new)
        return _dot_nt(p[:, :n_past].astype(BF16), vt) + _dot(p[:, n_past:].astype(BF16), vnew)

    nh = ckv_ref.shape[1]
    n_i = lax.broadcasted_iota(jnp.int32, (1, nh), 1)
    dist = (past + t_new) - (n_i * CMP_STRIDE + (CMP_LEN - 1))
    ckv = ckv_ref[0]
    s = _dot_nt(q, ckv[:, :LANES]) - slope * dist.astype(F32)
    pn = _masked_softmax(s, (dist >= 0) & (n_i < n_cmp))
    o_cmp = _dot(pn.astype(BF16), ckv[:, LANES:])
    psum = jnp.concatenate(
        [sum(pn[(k * GQA + g) * T8:(k * GQA + g + 1) * T8] for g in range(GQA)) for k in range(N_KV_HEADS)], axis=0)
    hi, lw = _split_bf16(psum)
    imp = _dot(hi, ov_ref[...]) + _dot(lw, ov_ref[...])
    n_blk = past // SEL_BLOCK
    lane = lax.broadcasted_iota(jnp.int32, (N_KV_HEADS * T8, LANES), 1)
    forced = (lane == 0) | (lane == n_blk - 1)
    score = jnp.where(forced, BIG, jnp.where(lane < n_blk, imp, -BIG))
    cnt = jnp.zeros(score.shape, F32)
    for jp in range(n_blk):
        r = score[:, jp:jp + 1]
        cnt = cnt + _ones_where(r > score) + jnp.where(r == score, _ones_where(lane > jp), 0.0)
    bias = jnp.where(cnt < float(N_SEL - 1), 0.0, NEG).astype(BF16)
    selbias = jnp.concatenate([bias[k * T8:(k + 1) * T8] for k in range(N_KV_HEADS) for _ in range(GQA)], axis=0)
    kt = buf_ref[slot, 0].astype(BF16)
    vt = buf_ref[slot, 1].astype(BF16)
    q_aug = jnp.concatenate([q, selbias], axis=1)
    k_aug = jnp.concatenate([kt, e_ref[...]], axis=0)
    o_slc = attend(kt, vt, tail(new[:, 2 * KV_W:3 * KV_W]), tail(new[:, 3 * KV_W:4 * KV_W]), (q_aug, k_aug), past)
    nw = win_ref.shape[3]
    o_win = attend(win_ref[0, 0].astype(BF16), win_ref[0, 1].astype(BF16),
                   tail(new[:, 4 * KV_W:5 * KV_W]), tail(new[:, 5 * KV_W:6 * KV_W]), None, nw)
    gates = gate_ref[0]
    o = gates[:, 0:1] * o_cmp + gates[:, 1:2] * o_slc + gates[:, 2:3] * o_win
    o_r = pltpu.roll(o, HALF, axis=1)
    lo = lax.broadcasted_iota(jnp.int32, (T8, LANES), 1) < HALF
    groups = []
    for k in range(N_KV_HEADS):
        for i in range(GQA // 2):
            r_even = slice((k * GQA + 2 * i) * T8, (k * GQA + 2 * i + 1) * T8)
            r_odd = slice((k * GQA + 2 * i + 1) * T8, (k * GQA + 2 * i + 2) * T8)
            groups.append(jnp.where(lo, o[r_even], o_r[r_odd]) if k == 0 else jnp.where(lo, o_r[r_even], o[r_odd]))
    o_ref[0] = _rms_norm(jnp.concatenate(groups, axis=1), go_ref[...]).astype(BF16)


def _nsa_sample(page_table, cache_t, q_bd, gate_rows, ckv, win_t, new8, e_mat, ov, go, past, n_new):
    nseq, n_pages = page_table.shape
    assert past == n_pages * PAGE_SIZE and past % SEL_BLOCK == 0 and past // SEL_BLOCK <= LANES
    assert n_new <= T8 and T8 % n_new == 0 and past % CMP_STRIDE == 0
    per_seq = lambda a: pl.BlockSpec((1,) + a.shape[1:], lambda i, pt: (i,) + (0,) * (a.ndim - 1))
    cs = lambda a: pl.BlockSpec(a.shape, lambda i, pt: (0,) * a.ndim, pipeline_mode=pl.Buffered(1))
    return pl.pallas_call(
        functools.partial(_nsa_sample_body, n_pages=n_pages, past=past, n_new=n_new,
                          n_cmp=past // CMP_STRIDE - 1),
        grid_spec=pltpu.PrefetchScalarGridSpec(
            num_scalar_prefetch=1, grid=(nseq,),
            in_specs=[pl.BlockSpec(memory_space=pl.ANY), per_seq(q_bd), per_seq(gate_rows), per_seq(ckv),
                      per_seq(win_t), per_seq(new8), cs(e_mat), cs(ov), cs(go)],
            out_specs=pl.BlockSpec((1, T8, ATTN_W), lambda i, pt: (i, 0, 0)),
            scratch_shapes=[pltpu.VMEM((2, 2, KV_W, past), F32), pltpu.SemaphoreType.DMA((2,))]),
        out_shape=jax.ShapeDtypeStruct((nseq, T8, ATTN_W), BF16),
        compiler_params=_params(1), name="nsa_sample",
    )(page_table, cache_t, q_bd, gate_rows, ckv, win_t, new8, e_mat, ov, go)


def _overlap(n_rows, n_cmp, lane0):
    n = np.arange(n_rows)[:, None]
    j = np.arange(LANES)[None, :] - lane0
    c_end = n * CMP_STRIDE + (CMP_LEN - 1)
    ov = (j * SEL_BLOCK <= c_end) & (j * SEL_BLOCK + SEL_BLOCK > c_end + 1 - CMP_LEN) & (n < n_cmp) & (j >= 0)
    return jnp.asarray(ov, BF16)


def _compress_weights(k_pos, k_w1, k_b1, k_w2, v_pos, v_w1, v_b1, v_w2):
    eye = jnp.eye(4, dtype=F32)
    feat = 4 * HEAD_DIM
    w1 = jnp.stack([k_w1, k_w1, v_w1, v_w1])
    half = lambda a: jnp.einsum("cldh,ce->lcdeh", a, eye).reshape(CMP_STRIDE, feat, 4 * CMP_HIDDEN)
    wab = jnp.concatenate([half(w1[:, :CMP_STRIDE]), half(w1[:, CMP_STRIDE:])], axis=2).astype(BF16)
    pos = jnp.stack([k_pos, k_pos, v_pos, v_pos])
    flat = lambda a: jnp.broadcast_to(jnp.transpose(a, (1, 0, 2)).reshape(CMP_STRIDE, 1, feat), (CMP_STRIDE, 8, feat))
    pos16 = jnp.concatenate([flat(pos[:, :CMP_STRIDE]), flat(pos[:, CMP_STRIDE:])], axis=1).astype(BF16)
    b1 = jnp.concatenate([k_b1, k_b1, v_b1, v_b1]).reshape(1, -1)
    z = jnp.zeros((CMP_HIDDEN, HEAD_DIM), F32)
    w2p = jnp.concatenate([
        jnp.concatenate([k_w2, z, z, z, z, z, z, z], axis=1), jnp.concatenate([z, z, k_w2, z, z, z, z, z], axis=1),
        jnp.concatenate([z, z, z, z, v_w2, z, z, z], axis=1), jnp.concatenate([z, z, z, z, z, z, v_w2, z], axis=1),
    ], axis=0).astype(BF16)
    w2s = jnp.concatenate([
        jnp.concatenate([k_w2, z, z, z], axis=1), jnp.concatenate([z, k_w2, z, z], axis=1),
        jnp.concatenate([z, z, v_w2, z], axis=1), jnp.concatenate([z, z, z, v_w2], axis=1)], axis=0).astype(BF16)
    return wab, pos16, b1, w2p, w2s


def _proj_weight(w_in):
    d = w_in.shape[0]
    g0 = ATTN_W + 6 * KV_W
    wg = jnp.concatenate([w_in[:, g0:g0 + 3 * N_HEADS], jnp.zeros((d, LANES - 3 * N_HEADS), F32)], axis=1)
    return jnp.concatenate([w_in[:, :ATTN_W] * HEAD_DIM ** -0.5, w_in[:, ATTN_W:g0], wg,
                            w_in[:, g0 + 3 * N_HEADS:]], axis=1).astype(BF16)


def _feature_major(a):
    lead = a.shape[:-4]
    n = len(lead)
    a = jnp.transpose(a, tuple(range(n)) + (n + 1, n + 2, n + 3, n))
    return a.reshape(lead + (2, KV_W, a.shape[-1]))


def _token_major(a, nb):
    return jnp.transpose(a.reshape(nb, 2, N_KV_HEADS, HEAD_DIM, a.shape[-1]), (0, 4, 1, 2, 3))


def kernel(x_prompt, x_sample, cache_cmp_kv, cache_slc_kv, state_win_kv, page_table, ln1_g, ln1_b, ffn1_w_in, ffn1_w_out, w_in, cmp_k_pos, cmp_k_w1, cmp_k_b1, cmp_k_w2, cmp_v_pos, cmp_v_w1, cmp_v_b1, cmp_v_w2, sgu_ln_g, sgu_ln_b, sgu_w, sgu_b, out_norm_g, w_out, ln2_g, ln2_b, ffn2_w_in, ffn2_w_out, ln3_g, ln3_b):
    depth = w_in.shape[0]
    nb, seq, d = x_prompt.shape
    ns, n_new, _ = x_sample.shape
    n_pages = page_table.shape[1]
    past = n_pages * PAGE_SIZE
    d_ff = ffn1_w_out.shape[1]
    alpha = (2.0 * depth) ** 0.25
    row = lambda a: a.reshape(1, -1)
    hp = x_prompt.reshape(nb * seq, d)
    hs = x_sample.reshape(ns * n_new, d)
    kv5 = (N_KV_HEADS, HEAD_DIM)
    outs = [[] for _ in range(7)]
    ov_p = _overlap(seq // CMP_STRIDE, seq // CMP_STRIDE - 1, HALF)
    ov_s = _overlap(past // CMP_STRIDE, past // CMP_STRIDE - 1, 0)
    e_mat = jnp.asarray(np.arange(LANES)[:, None] == np.arange(past)[None, :] // SEL_BLOCK, BF16)
    for l in range(depth):
        f1 = (ffn1_w_in[l, :, :d_ff].astype(BF16), ffn1_w_in[l, :, d_ff:].astype(BF16), ffn1_w_out[l].astype(BF16),
              row(ln1_g[l]), row(ln1_b[l]))
        f2 = (ffn2_w_in[l, :, :d_ff].astype(BF16), ffn2_w_in[l, :, d_ff:].astype(BF16), ffn2_w_out[l].astype(BF16),
              row(ln3_g[l]), row(ln3_b[l]))
        wp = _proj_weight(w_in[l])
        wab, pos16, b1, w2p, w2s = _compress_weights(cmp_k_pos[l], cmp_k_w1[l], cmp_k_b1[l], cmp_k_w2[l],
                                                     cmp_v_pos[l], cmp_v_w1[l], cmp_v_b1[l], cmp_v_w2[l])
        sg, sb = row(sgu_ln_g[l]), row(sgu_ln_b[l])
        go_a, go_g = row(out_norm_g[l, :ATTN_W]), row(out_norm_g[l, ATTN_W:])
        wm = w_out[l].astype(BF16)
        mix_p = jnp.tril(sgu_w[l]).astype(BF16)
        mb_p = jnp.repeat(sgu_b[l].T, GMLP_DIM, axis=1)
        r = min(n_new, CHUNK)
        mix_s = jax.vmap(lambda w: jnp.kron(jnp.eye(CHUNK // r, dtype=F32), w))(jnp.tril(sgu_w[l, :, :r, :r])).astype(BF16)
        mb_s = jnp.tile(jnp.repeat(sgu_b[l, :, :r].T, GMLP_DIM, axis=1), (CHUNK // r, 1))

        hp = _ffn_ln(hp, *f1, alpha)
        (qpad, cmpk_p, cmpv_p, _, _, cmp_t, slc_t, win_t, ksa, vsd, kwp, vwd, gate, ogn, _) = _proj(
            hp, wp, sg, sb, mix_p, mb_p, go_g, seq)
        ckv = _compress_prompt(cmpk_p, cmpv_p, nb, wab, pos16, b1, w2p)
        oan = _nsa_prompt(qpad, gate, ckv, ksa, vsd, kwp, vwd, ov_p, go_a, nb, seq)
        hp = _ffn_ln(hp, *f2, alpha, merge_args=(oan, ogn, wm, row(ln2_g[l]), row(ln2_b[l])))
        outs[0].append(_token_major(cmp_t, nb))
        outs[2].append(_token_major(slc_t, nb))
        outs[4].append(_token_major(win_t[:, :, seq - min(WINDOW, seq):], nb))

        hs = _ffn_ln(hs, *f1, alpha)
        (qpad, cmpk_s, cmpv_s, slc_s, win_s, _, _, _, _, _, _, _, gate, ogn, v_s) = _proj(
            hs, wp, sg, sb, mix_s, mb_s, go_g, ns * n_new)
        cmp_s = jnp.concatenate([cmpk_s, cmpv_s], axis=1)
        ckv = _compress_sample(page_table, _feature_major(cache_cmp_kv[l]), wab, pos16, b1, w2s)
        q5 = qpad.reshape(ns, n_new, N_KV_HEADS, GQA, LANES)[..., :HEAD_DIM]
        q5 = jnp.pad(jnp.transpose(q5, (0, 2, 3, 1, 4)), ((0, 0), (0, 0), (0, 0), (0, T8 - n_new), (0, 0)))
        zq = jnp.zeros_like(q5[:, 0])
        q_bd = jnp.stack([jnp.concatenate([q5[:, 0], zq], axis=-1), jnp.concatenate([zq, q5[:, 1]], axis=-1)],
                         axis=1).reshape(ns, N_KV_HEADS * GQA * T8, LANES)
        g5 = gate[:, :3 * N_HEADS].reshape(ns, n_new, 3, N_KV_HEADS, GQA)
        g5 = jnp.pad(jnp.transpose(g5, (0, 3, 4, 1, 2)), ((0, 0), (0, 0), (0, 0), (0, T8 - n_new), (0, 0)))
        gate_rows = g5.reshape(ns, N_KV_HEADS * GQA * T8, 3)
        new8 = jnp.pad(jnp.concatenate([cmp_s, slc_s, win_s], axis=1).reshape(ns, n_new, 6 * KV_W),
                       ((0, 0), (0, T8 - n_new), (0, 0)))
        oan = _nsa_sample(page_table, _feature_major(cache_slc_kv[l]), q_bd, gate_rows, ckv,
                          _feature_major(state_win_kv[l]), new8, e_mat, ov_s, go_a, past, n_new)
        oan = oan[:, :n_new].reshape(ns * n_new, ATTN_W)
        hs = _ffn_ln(hs, *f2, alpha, merge_args=(oan, ogn, wm, row(ln2_g[l]), row(ln2_b[l])))
        win_all = jnp.concatenate([state_win_kv[l], win_s.reshape(ns, n_new, 2, *kv5)], axis=1)
        outs[1].append(cmp_s.reshape(ns, n_new, 2, *kv5))
        outs[3].append(slc_s.reshape(ns, n_new, 2, *kv5))
        outs[5].append(win_all[:, -min(WINDOW, past + n_new):])
        outs[6].append(v_s.reshape(ns, n_new, N_GMLP, GMLP_DIM))
    st = [jnp.stack(o, axis=0) for o in outs]
    return (hp.reshape(nb, seq, d), hs.reshape(ns, n_new, d), st[0], st[1], st[2], st[3], st[4], st[5], st[6])
```

```python
import functools

import numpy as np
import jax
import jax.numpy as jnp
from jax import lax
from jax.experimental import pallas as pl
from jax.experimental.pallas import tpu as pltpu

HEAD_DIM = 64
N_HEADS = 8
N_KV_HEADS = 2
GQA = N_HEADS // N_KV_HEADS
ATTN_W = N_HEADS * HEAD_DIM
KV_W = N_KV_HEADS * HEAD_DIM
CMP_LEN = 32
CMP_STRIDE = 16
CMP_HIDDEN = 64
SEL_BLOCK = 64
N_SEL = 16
WINDOW = 512
N_GMLP = 8
GMLP_DIM = 64
GMLP_W = N_GMLP * GMLP_DIM
CHUNK = 128
PAGE_SIZE = 128
LN_EPS = 1e-5
NEG = -1e30
BIG = 1e9

LANES = 128
HALF = LANES // 2
VMEM_LIMIT = 56 * 1024 * 1024
SLOPES = [2.0 ** -(h + 1) for h in range(N_HEADS)]

F32 = jnp.float32
BF16 = jnp.bfloat16

_Q0, _KV0, _G0, _U0, _V0, _PW = 0, ATTN_W, ATTN_W + 6 * KV_W, ATTN_W + 6 * KV_W + LANES, \
    ATTN_W + 6 * KV_W + LANES + GMLP_W, ATTN_W + 6 * KV_W + LANES + 2 * GMLP_W


def _dot(a, b):
    return jnp.dot(a, b, preferred_element_type=F32)


def _dot_nt(a, b):
    return lax.dot_general(a, b, (((1,), (1,)), ((), ())), preferred_element_type=F32)


def _layer_norm(y, g, b):
    mu = jnp.mean(y, axis=-1, keepdims=True)
    d = y - mu
    var = jnp.mean(d * d, axis=-1, keepdims=True)
    return d * lax.rsqrt(var + LN_EPS) * g + b


def _rms_norm(y, g):
    return y * lax.rsqrt(jnp.mean(y * y, axis=-1, keepdims=True) + LN_EPS) * g


def _ones_where(mask):
    return jnp.where(mask, 1.0, 0.0)


def _alibi_lanes(lane, hi, lo):
    return jnp.where(lane == HALF, hi, jnp.where(lane == HALF + 1, lo, 0.0))


def _log2(n):
    assert n > 0 and n & (n - 1) == 0, n
    return n.bit_length() - 1


def _const_spec(shape):
    nd = len(shape)
    return pl.BlockSpec(shape, lambda *_: (0,) * nd, pipeline_mode=pl.Buffered(1))


def _params(n_grid):
    return pltpu.CompilerParams(dimension_semantics=("arbitrary",) * n_grid, vmem_limit_bytes=VMEM_LIMIT)


def _ffn_body(*refs, alpha, merge):
    if merge:
        (h_ref, oa_ref, og_ref, wm_ref, gm_ref, bm_ref,
         wa_ref, wb_ref, wo_ref, g_ref, b_ref, o_ref) = refs
        cat = jnp.concatenate([oa_ref[...], og_ref[...]], axis=1)
        x = _layer_norm(alpha * h_ref[...] + _dot(cat, wm_ref[...]), gm_ref[...], bm_ref[...])
    else:
        x_ref, wa_ref, wb_ref, wo_ref, g_ref, b_ref, o_ref = refs
        x = x_ref[...]
    xb = x.astype(BF16)
    a = _dot(xb, wa_ref[...])
    b = _dot(xb, wb_ref[...])
    hid = ((a * jax.nn.sigmoid(a)) * b).astype(BF16)
    y = alpha * x + 0.5 * _dot(hid, wo_ref[...])
    o_ref[...] = _layer_norm(y, g_ref[...], b_ref[...])


def _ffn_ln(x, wa, wb, wo, g, b, alpha, merge_args=None, tm=512):
    t, d = x.shape
    tm = min(tm, t)
    assert t % tm == 0
    row = lambda w: pl.BlockSpec((tm, w), lambda i: (i, 0))
    ins, specs = [x], [row(d)]
    if merge_args is not None:
        oa, og, wm, gm, bm = merge_args
        ins += [oa, og, wm, gm, bm]
        specs += [row(oa.shape[1]), row(og.shape[1]), _const_spec(wm.shape), _const_spec(gm.shape),
                  _const_spec(bm.shape)]
    ins += [wa, wb, wo, g, b]
    specs += [_const_spec(a.shape) for a in (wa, wb, wo, g, b)]
    return pl.pallas_call(
        functools.partial(_ffn_body, alpha=alpha, merge=merge_args is not None),
        grid=(t // tm,), in_specs=specs, out_specs=row(d),
        out_shape=jax.ShapeDtypeStruct((t, d), F32), compiler_params=_params(1),
        name="ffn_merge_ln" if merge_args is not None else "ffn_ln",
    )(*ins)


def _proj_body(h_ref, w_ref, sg_ref, sb_ref, mix_ref, mb_ref, go_ref,
               qpad_ref, cmpk_ref, cmpv_ref, slc_ref, win_ref, cmpt_ref, slct_ref, wint_ref,
               ksa_ref, vsd_ref, kwp_ref, vwd_ref, gate_ref, ogn_ref, v_ref, *, tm, seq):
    p = _dot(h_ref[...].astype(BF16), w_ref[...])
    lane = lax.broadcasted_iota(jnp.int32, (tm, LANES), 1)
    lo = lane < HALF
    tok = lax.rem(pl.program_id(0) * tm, seq) + lax.broadcasted_iota(jnp.int32, (tm, LANES), 0)
    for i in range(ATTN_W // LANES):
        x = p[:, _Q0 + i * LANES:_Q0 + (i + 1) * LANES]
        xr = pltpu.roll(x, HALF, axis=1)
        qpad_ref[:, (2 * i) * LANES:(2 * i + 1) * LANES] = jnp.where(
            lo, x, _alibi_lanes(lane, LANES * SLOPES[2 * i], SLOPES[2 * i])).astype(BF16)
        qpad_ref[:, (2 * i + 1) * LANES:(2 * i + 2) * LANES] = jnp.where(
            lo, xr, _alibi_lanes(lane, LANES * SLOPES[2 * i + 1], SLOPES[2 * i + 1])).astype(BF16)
    for s, feat_ref in enumerate((cmpt_ref, slct_ref, wint_ref)):
        feat_ref[0] = p[:, _KV0 + 2 * s * KV_W:_KV0 + 2 * (s + 1) * KV_W].T
    cmpk_ref[...] = p[:, _KV0:_KV0 + KV_W]
    cmpv_ref[...] = p[:, _KV0 + KV_W:_KV0 + 2 * KV_W]
    slc_ref[...] = p[:, _KV0 + 2 * KV_W:_KV0 + 4 * KV_W]
    win_ref[...] = p[:, _KV0 + 4 * KV_W:_KV0 + 6 * KV_W]
    onehot = _ones_where(lane - HALF == (tok >> _log2(SEL_BLOCK)))
    ks = p[:, _KV0 + 2 * KV_W:_KV0 + 3 * KV_W]
    ksr = pltpu.roll(ks, HALF, axis=1)
    ksa_ref[:, :LANES] = jnp.where(lo, ks, onehot).astype(BF16)
    ksa_ref[:, LANES:] = jnp.where(lo, ksr, onehot).astype(BF16)
    vs = p[:, _KV0 + 3 * KV_W:_KV0 + 4 * KV_W]
    vsr = pltpu.roll(vs, HALF, axis=1)
    vsd_ref[:, :LANES] = jnp.where(lo, vs, 1.0).astype(BF16)
    vsd_ref[:, LANES:] = jnp.where(lo, vsr, 1.0).astype(BF16)
    kw = p[:, _KV0 + 4 * KV_W:_KV0 + 5 * KV_W]
    kwr = pltpu.roll(kw, HALF, axis=1)
    pos_pair = _alibi_lanes(lane, (tok >> _log2(LANES)).astype(F32), (tok & (LANES - 1)).astype(F32))
    kwp_ref[:, :LANES] = jnp.where(lo, kw, pos_pair).astype(BF16)
    kwp_ref[:, LANES:] = jnp.where(lo, kwr, pos_pair).astype(BF16)
    vw = p[:, _KV0 + 5 * KV_W:_KV0 + 6 * KV_W]
    vwr = pltpu.roll(vw, HALF, axis=1)
    vwd_ref[:, :LANES] = jnp.where(lo, vw, 1.0).astype(BF16)
    vwd_ref[:, LANES:] = jnp.where(lo, vwr, 1.0).astype(BF16)
    gate_ref[...] = jax.nn.sigmoid(p[:, _G0:_G0 + LANES])
    u = jax.nn.gelu(p[:, _U0:_U0 + GMLP_W])
    v = _layer_norm(jax.nn.gelu(p[:, _V0:_V0 + GMLP_W]), sg_ref[...], sb_ref[...])
    v_ref[...] = v
    vb = v.astype(BF16)
    lo_c = lax.broadcasted_iota(jnp.int32, (CHUNK, LANES), 1) < HALF
    for c in range(tm // CHUNK):
        rows = slice(c * CHUNK, (c + 1) * CHUNK)
        zs = []
        for i in range(GMLP_W // LANES):
            vp = vb[rows, i * LANES:(i + 1) * LANES]
            zs.append(jnp.where(lo_c, _dot(mix_ref[2 * i], vp), _dot(mix_ref[2 * i + 1], vp)))
        z = jnp.concatenate(zs, axis=1) + mb_ref[...]
        ogn_ref[rows, :] = _rms_norm(u[rows, :] * z, go_ref[...]).astype(BF16)


def _proj(h, w, sg, sb, mix, mb, go, seq, tm=512):
    t, d = h.shape
    tm = min(tm, t)
    assert t % tm == 0 and tm % CHUNK == 0 and seq % tm == 0 and t % seq == 0 and seq // SEL_BLOCK <= HALF
    nq = seq // tm
    row = lambda width: pl.BlockSpec((tm, width), lambda i: (i, 0))
    feat = pl.BlockSpec((1, 2 * KV_W, tm), lambda i: (i // nq, 0, i % nq))
    tok_major = [(2 * ATTN_W, BF16), (KV_W, F32), (KV_W, F32), (2 * KV_W, F32), (2 * KV_W, F32)]
    rest = [(2 * LANES, BF16), (2 * LANES, BF16), (2 * LANES, BF16), (2 * LANES, BF16), (LANES, F32),
            (GMLP_W, BF16), (GMLP_W, F32)]
    sds = jax.ShapeDtypeStruct
    return pl.pallas_call(
        functools.partial(_proj_body, tm=tm, seq=seq),
        grid=(t // tm,),
        in_specs=[row(d)] + [_const_spec(a.shape) for a in (w, sg, sb, mix, mb, go)],
        out_specs=[row(wd) for wd, _ in tok_major] + [feat] * 3 + [row(wd) for wd, _ in rest],
        out_shape=[sds((t, wd), dt) for wd, dt in tok_major] + [sds((t // seq, 2 * KV_W, seq), F32)] * 3
        + [sds((t, wd), dt) for wd, dt in rest],
        compiler_params=_params(1), name="proj",
    )(h, w, sg, sb, mix, mb, go)


def _compress_math(rows_at, nh, wab_ref, pos_ref, b1_ref, w2_ref):
    hw = b1_ref.shape[1]
    ab = None
    for l in range(CMP_STRIDE):
        lhs = jnp.concatenate([rows_at(l).astype(BF16), pos_ref[l]], axis=0)
        d = _dot(lhs, wab_ref[l])
        ab = d if ab is None else ab + d
    c = ab[nh:nh + 1, :hw] + ab[nh + 8:nh + 9, hw:] + b1_ref[...]
    hid = ab[:nh, :hw] + pltpu.roll(ab[:nh, hw:], nh - 1, axis=0) + c
    return _dot(jax.nn.gelu(hid).astype(BF16), w2_ref[...])


def _strided_rows(k_ref, v_ref, nh):
    return lambda l: jnp.concatenate([k_ref[pl.ds(l, nh, stride=CMP_STRIDE), :],
                                      v_ref[pl.ds(l, nh, stride=CMP_STRIDE), :]], axis=1)


def _compress_body(xk_ref, xv_ref, wab_ref, pos_ref, b1_ref, w2_ref, o_ref):
    nh = o_ref.shape[1]
    out = _compress_math(_strided_rows(xk_ref, xv_ref, nh), nh, wab_ref, pos_ref, b1_ref, w2_ref)
    lane = lax.broadcasted_iota(jnp.int32, (nh, LANES), 1)
    c_end = lax.broadcasted_iota(jnp.int32, (nh, LANES), 0) * CMP_STRIDE + (CMP_LEN - 1)
    pair = _alibi_lanes(lane, (c_end >> _log2(LANES)).astype(F32), (c_end & (LANES - 1)).astype(F32))
    for k in range(N_KV_HEADS):
        o_ref[0, :, k * LANES:(k + 1) * LANES] = (out[:, k * LANES:(k + 1) * LANES] + pair).astype(o_ref.dtype)
    o_ref[0, :, N_KV_HEADS * LANES:] = out[:, N_KV_HEADS * LANES:].astype(o_ref.dtype)


def _compress_prompt(xk, xv, nb, wab, pos, b1, w2):
    seq = xk.shape[0] // nb
    nh = seq // CMP_STRIDE
    return pl.pallas_call(
        _compress_body, grid=(nb,),
        in_specs=[pl.BlockSpec((seq, KV_W), lambda i: (i, 0))] * 2 + [_const_spec(a.shape) for a in (wab, pos, b1, w2)],
        out_specs=pl.BlockSpec((1, nh, w2.shape[1]), lambda i: (i, 0, 0)),
        out_shape=jax.ShapeDtypeStruct((nb, nh, w2.shape[1]), BF16),
        compiler_params=_params(1), name="compress_prompt",
    )(xk, xv, wab, pos, b1, w2)


def _gather_step(make_copies):
    i, n = pl.program_id(0), pl.num_programs(0)
    slot = lax.rem(i, 2)

    @pl.when(i == 0)
    def _():
        for cp in make_copies(i, slot):
            cp.start()

    @pl.when(i + 1 < n)
    def _():
        for cp in make_copies(i + 1, 1 - slot):
            cp.start()

    for cp in make_copies(i, slot):
        cp.wait()
    return slot


def _compress_sample_body(pt_ref, cache_ref, perm_ref, wab_ref, pos_ref, b1_ref, w2_ref, o_ref, buf_ref, xs_ref,
                          sem_ref, *, n_pages):
    def make_copies(seq_idx, slot):
        return [pltpu.make_async_copy(cache_ref.at[pt_ref[seq_idx, j]], buf_ref.at[slot, j], sem_ref.at[slot])
                for j in range(n_pages)]

    slot = _gather_step(make_copies)
    per_page = PAGE_SIZE // CMP_STRIDE

    feat, page = buf_ref.shape[3], buf_ref.shape[4]
    xt = buf_ref[slot].reshape(n_pages * 2 * feat, page)
    z = _dot_nt(perm_ref[...], xt.astype(BF16))
    for j in range(n_pages):
        for c in range(2):
            col = slice((2 * j + c) * feat, (2 * j + c + 1) * feat)
            for l in range(CMP_STRIDE):
                xs_ref[c, l, j * per_page:(j + 1) * per_page, :] = z[l * per_page:(l + 1) * per_page, col]
    nh = o_ref.shape[1]
    rows_at = lambda l: jnp.concatenate([xs_ref[0, l], xs_ref[1, l]], axis=1)
    o_ref[0] = _compress_math(rows_at, nh, wab_ref, pos_ref, b1_ref, w2_ref).astype(o_ref.dtype)


def _compress_sample(page_table, cache_t, wab, pos, b1, w2):
    nseq, n_pages = page_table.shape
    _, _, feat, page = cache_t.shape
    nh = n_pages * page // CMP_STRIDE
    cs = lambda a: pl.BlockSpec(a.shape, lambda i, pt: (0,) * a.ndim, pipeline_mode=pl.Buffered(1))
    tok = np.arange(page)
    perm = jnp.asarray((tok % CMP_STRIDE * (page // CMP_STRIDE) + tok // CMP_STRIDE)[None, :] == tok[:, None], BF16)
    return pl.pallas_call(
        functools.partial(_compress_sample_body, n_pages=n_pages),
        grid_spec=pltpu.PrefetchScalarGridSpec(
            num_scalar_prefetch=1, grid=(nseq,),
            in_specs=[pl.BlockSpec(memory_space=pl.ANY)] + [cs(a) for a in (perm, wab, pos, b1, w2)],
            out_specs=pl.BlockSpec((1, nh, w2.shape[1]), lambda i, pt: (i, 0, 0)),
            scratch_shapes=[pltpu.VMEM((2, n_pages, 2, feat, page), F32), pltpu.VMEM((2, CMP_STRIDE, nh, feat), F32),
                            pltpu.SemaphoreType.DMA((2,))]),
        out_shape=jax.ShapeDtypeStruct((nseq, nh, w2.shape[1]), BF16),
        compiler_params=_params(1), name="compress_sample",
    )(page_table, cache_t, perm, wab, pos, b1, w2)


def _masked_softmax(s, mask):
    sm = jnp.where(mask, s, NEG)
    m = jnp.max(sm, axis=-1, keepdims=True)
    p = jnp.where(mask, jnp.exp(sm - m), 0.0)
    l = jnp.sum(p, axis=-1, keepdims=True)
    return p * (1.0 / jnp.maximum(l, 1e-30))


def _split_bf16(x):
    hi = x.astype(BF16)
    return hi, (x - hi.astype(F32)).astype(BF16)


def _alibi_slopes(head_idx):
    out = jnp.zeros(head_idx.shape, F32)
    for h in range(N_HEADS):
        out = jnp.where(head_idx == h, SLOPES[h], out)
    return out


def _nsa_prompt_body(q_ref, gate_ref, ckv_ref, ksa_ref, vsd_ref, kwp_ref, vwd_ref, ov_ref, go_ref,
                     o_ref, s_ref, qsel_ref, mrun_ref, mb_ref, acc_ref, tile_ref, *, tq, tk, nwin, q_lo, nblk):
    n_tiles = nblk * SEL_BLOCK // tk
    t0 = (pl.program_id(1) + q_lo) * tq
    rows = GQA * tq
    lane = lax.broadcasted_iota(jnp.int32, (tq, LANES), 1)
    lo = lane < HALF
    t_rel = lax.broadcasted_iota(jnp.int32, (rows, 1), 0) & (tq - 1)
    cur = (t0 + lax.broadcasted_iota(jnp.int32, (tq, 1), 0)) >> _log2(SEL_BLOCK)
    sub = lax.broadcasted_iota(jnp.int32, (8, tq), 0)
    gates = gate_ref[...]
    nh = nblk * SEL_BLOCK // CMP_STRIDE
    c_rel =lax.broadcasted_iota(jnp.int32, (1, nh), 1) * CMP_STRIDE + (CMP_LEN - 1) - t0
    n_full = t0 // tk
    w_start = pl.multiple_of(jnp.maximum(t0 - WINDOW, 0), tq)
    w_rel = (w_start - t0) + lax.broadcasted_iota(jnp.int32, (1, nwin), 1)
    t_q = lax.broadcasted_iota(jnp.int32, (tq, 1), 0)
    w_bias = jnp.where((w_rel <= t_q) & (w_rel >= t_q - WINDOW), 0.0, NEG)
    cols = [slice(k * LANES, (k + 1) * LANES) for k in range(N_KV_HEADS)]
    blk_tile = lax.broadcasted_iota(jnp.int32, (nblk, 1), 0) >> _log2(tk // SEL_BLOCK)

    def add_alibi(s, kvh, rel):
        relf = rel.astype(F32)
        return jnp.concatenate([s[g * tq:(g + 1) * tq] + SLOPES[kvh * GQA + g] * relf for g in range(GQA)], axis=0)

    o_cmp, o_win = [], []
    tile_hits = [jnp.int32(0)] * n_tiles
    for kvh in range(N_KV_HEADS):
        qs = [q_ref[:, (kvh * GQA + g) * LANES:(kvh * GQA + g + 1) * LANES] for g in range(GQA)]
        qall = jnp.concatenate(qs, axis=0)
        pn = _masked_softmax(_dot_nt(qall, ckv_ref[0, :nh, cols[kvh]]), c_rel <= t_rel)
        o_cmp.append(_dot(pn.astype(BF16), ckv_ref[0, :nh, 2 * LANES + kvh * LANES:2 * LANES + (kvh + 1) * LANES]))
        psum = pn[0:tq] + pn[tq:2 * tq] + pn[2 * tq:3 * tq] + pn[3 * tq:4 * tq]
        hi, lw = _split_bf16(psum)
        imp = _dot(hi, ov_ref[:nh, :]) + _dot(lw, ov_ref[:nh, :])
        j = lane - HALF
        forced = (j == 0) | (j == cur) | (j == cur - 1)
        score = jnp.where(forced, BIG, jnp.where(j <= cur, imp, -BIG))
        sc_t = score.T[HALF:HALF + nblk, :]
        cnt = [jnp.zeros((8, tq), F32) for _ in range(nblk // 8)]
        for jp in range(nblk):
            r = sc_t[jp:jp + 1, :]
            for v in range(nblk // 8):
                blk = sc_t[8 * v:8 * v + 8, :]
                if 8 * v > jp:
                    beats = _ones_where(r >= blk)
                elif 8 * v + 7 <= jp:
                    beats = _ones_where(r > blk)
                else:
                    beats = _ones_where(r > blk) + jnp.where(r == blk, _ones_where(sub > jp - 8 * v), 0.0)
                cnt[v] = cnt[v] + beats
        sel = jnp.concatenate(cnt, axis=0) < float(N_SEL)
        never = [jnp.full((HALF - nblk, tq), NEG, F32)] if nblk < HALF else []
        selbias = jnp.concatenate([jnp.zeros((HALF, tq), F32), jnp.where(sel, 0.0, NEG)] + never, axis=0).T.astype(BF16)
        qsel_ref[kvh] = jnp.concatenate([jnp.where(lo, q, selbias) for q in qs], axis=0)
        any_sel = jnp.max(_ones_where(sel), axis=-1, keepdims=True)
        for w in range(-(-n_tiles // 8)):
            weight = jnp.zeros((nblk, 1), F32)
            for k in range(8):
                weight = jnp.where(blk_tile == 8 * w + k, float(8 ** k), weight)
            packed = jnp.sum(any_sel * weight).astype(jnp.int32)
            for k in range(min(8, n_tiles - 8 * w)):
                tile_hits[8 * w + k] = tile_hits[8 * w + k] + ((packed >> (3 * k)) & 7)
        s = _dot_nt(qall, kwp_ref[pl.ds(w_start, nwin), cols[kvh]])
        s = jnp.concatenate([s[g * tq:(g + 1) * tq] + w_bias for g in range(GQA)], axis=0)
        pw = jnp.exp((s - jnp.max(s, axis=-1, keepdims=True)).astype(BF16))
        o_win.append(_dot(pw, vwd_ref[pl.ds(w_start, nwin), cols[kvh]]))
        mrun_ref[kvh] = jnp.full((rows, LANES), NEG, F32)
        acc_ref[kvh] = jnp.zeros((rows, LANES), F32)

    def scores(kvh, jt, causal):
        k0 = pl.multiple_of(jt * tk, tk)
        k_rel = (k0 - t0) + lax.broadcasted_iota(jnp.int32, (1, tk), 1)
        s = add_alibi(_dot_nt(qsel_ref[kvh], ksa_ref[pl.ds(k0, tk), cols[kvh]]), kvh, k_rel)
        if causal:
            s = jnp.where(k_rel <= t_rel, s, NEG)
        s_ref[kvh, jt] = s
        mt = s[:, :LANES]
        for c in range(1, tk // LANES):
            mt = jnp.maximum(mt, s[:, c * LANES:(c + 1) * LANES])
        mrun_ref[kvh] = jnp.maximum(mrun_ref[kvh], mt)

    n_hit = jnp.int32(0)
    for jt in range(n_tiles):
        tile_ref[n_hit] = jt
        n_hit = n_hit + ((tile_hits[jt] > 0) & (jt < n_full)).astype(jnp.int32)

    def sweep(visit):
        def pair(i, carry):
            for u in range(2):
                for kvh in range(N_KV_HEADS):
                    visit(kvh, tile_ref[2 * i + u])
            return carry

        lax.fori_loop(0, n_hit >> 1, pair, 0)

        @pl.when((n_hit & 1) == 1)
        def _():
            for kvh in range(N_KV_HEADS):
                visit(kvh, tile_ref[n_hit - 1])

    sweep(lambda kvh, jt: scores(kvh, jt, False))
    for kvh in range(N_KV_HEADS):
        scores(kvh, n_full, True)
        mb_ref[kvh] = jnp.broadcast_to(jnp.max(mrun_ref[kvh], axis=-1, keepdims=True), (rows, tk))

    def weigh(kvh, jt):
        k0 = pl.multiple_of(jt * tk, tk)
        p = jnp.exp((s_ref[kvh, jt] - mb_ref[kvh]).astype(BF16))
        acc_ref[kvh] = acc_ref[kvh] + _dot(p, vsd_ref[pl.ds(k0, tk), cols[kvh]])

    sweep(weigh)
    for kvh in range(N_KV_HEADS):
        weigh(kvh, n_full)
    head_out = []
    for kvh in range(N_KV_HEADS):
        gate_col = [jnp.concatenate([gates[:, b * N_HEADS + kvh * GQA + g:b * N_HEADS + kvh * GQA + g + 1]
                                     for g in range(GQA)], axis=0) for b in range(3)]
        pv_slc, pv_win = acc_ref[kvh], o_win[kvh]
        out = (gate_col[0] * o_cmp[kvh]
               + (gate_col[1] * (1.0 / pv_slc[:, HALF:HALF + 1])) * pv_slc
               + (gate_col[2] * (1.0 / pv_win[:, HALF:HALF + 1])) * pv_win)
        head_out += [out[g * tq:(g + 1) * tq] for g in range(GQA)]
    o = jnp.concatenate([jnp.where(lo, head_out[2 * i], pltpu.roll(head_out[2 * i + 1], HALF, axis=1))
                         for i in range(N_HEADS // 2)], axis=1)
    o_ref[...] = _rms_norm(o, go_ref[...]).astype(BF16)


def _nsa_prompt(qpad, gate, ckv, ksa, vsd, kwp, vwd, ov, go, nb, seq, tq=256, tk=256):
    assert seq % tk == 0 and tk % tq == 0 and tq % LANES == 0 and seq >= WINDOW + tq and tk // SEL_BLOCK <= 7
    nwin = WINDOW + tq
    nq = seq // tq
    seqblk = lambda w: pl.BlockSpec((seq, w), lambda b, i: (b, 0), pipeline_mode=pl.Buffered(1))
    cs = lambda a: pl.BlockSpec(a.shape, lambda b, i: (0,) * a.ndim, pipeline_mode=pl.Buffered(1))
    rows = GQA * tq
    acc = pltpu.VMEM((N_KV_HEADS, rows, LANES), F32)
    n_seg = 4 if nq % 4 == 0 and (seq // 4) % (8 * SEL_BLOCK) == 0 else 1
    per_seg = nq // n_seg
    outs = []
    for sgm in range(n_seg):
        q_lo = sgm * per_seg
        nblk = (q_lo + per_seg) * tq // SEL_BLOCK
        n_tiles = nblk * SEL_BLOCK // tk
        qrow = lambda w, q_lo=q_lo: pl.BlockSpec((tq, w), lambda b, i: (b * nq + q_lo + i, 0))
        outs.append(pl.pallas_call(
            functools.partial(_nsa_prompt_body, tq=tq, tk=tk, nwin=nwin, q_lo=q_lo, nblk=nblk),
            grid=(nb, per_seg),
            in_specs=[qrow(qpad.shape[1]), qrow(LANES), pl.BlockSpec((1,) + ckv.shape[1:], lambda b, i: (b, 0, 0)),
                      seqblk(2 * LANES), seqblk(2 * LANES), seqblk(2 * LANES), seqblk(2 * LANES), cs(ov), cs(go)],
            out_specs=pl.BlockSpec((tq, ATTN_W), lambda b, i: (b * per_seg + i, 0)),
            out_shape=jax.ShapeDtypeStruct((nb * per_seg * tq, ATTN_W), BF16),
            scratch_shapes=[pltpu.VMEM((N_KV_HEADS, n_tiles, rows, tk), F32),
                            pltpu.VMEM((N_KV_HEADS, rows, LANES), BF16),
                            acc, pltpu.VMEM((N_KV_HEADS, rows, tk), F32), acc,
                            pltpu.SMEM((n_tiles,), jnp.int32)],
            compiler_params=_params(2), name="nsa_prompt",
        )(qpad, gate, ckv, ksa, vsd, kwp, vwd, ov, go).reshape(nb, per_seg * tq, ATTN_W))
    return jnp.concatenate(outs, axis=1).reshape(nb * seq, ATTN_W)


T8 = 8


def _nsa_sample_body(pt_ref, cache_ref, q_ref, gate_ref, ckv_ref, win_ref, new_ref, e_ref, ov_ref, go_ref,
                     o_ref, buf_ref, sem_ref, *, n_pages, past, n_new, n_cmp):
    def make_copies(seq_idx, slot):
        return [pltpu.make_async_copy(cache_ref.at[pt_ref[seq_idx, j]],
                                      buf_ref.at[slot, :, :, pl.ds(j * PAGE_SIZE, PAGE_SIZE)], sem_ref.at[slot])
                for j in range(n_pages)]

    slot = _gather_step(make_copies)
    rows = N_KV_HEADS * GQA * T8
    ridx = lax.broadcasted_iota(jnp.int32, (rows, 1), 0)
    t_new = ridx & (n_new - 1)
    slope = _alibi_slopes(ridx >> _log2(T8))
    q = q_ref[0]
    new = new_ref[0]

    def tail(x):
        return jnp.concatenate([x, jnp.zeros((LANES - T8, x.shape[1]), F32)], axis=0).astype(BF16)

    def attend(kt, vt, knew, vnew, extra, n_past):
        rel = lax.broadcasted_iota(jnp.int32, (1, n_past + LANES), 1) - n_past
        s = jnp.concatenate([_dot(q, kt) if extra is None else _dot(*extra), _dot_nt(q, knew)], axis=1)
        s = s + slope * rel.astype(F32)
        p = _masked_softmax(s, (rel <= t_new) & (rel >= t_new - WINDOW) & (rel + past >= 0)
                            if extra is None else rel <= t_new)
        return _dot_nt(p[:, :n_past].astype(BF16), vt) + _dot(p[:, n_past:].astype(BF16), vnew)

    nh = ckv_ref.shape[1]
    n_i = lax.broadcasted_iota(jnp.int32, (1, nh), 1)
    dist = (past + t_new) - (n_i * CMP_STRIDE + (CMP_LEN - 1))
    ckv = ckv_ref[0]
    s = _dot_nt(q, ckv[:, :LANES]) - slope * dist.astype(F32)
    pn = _masked_softmax(s, (dist >= 0) & (n_i < n_cmp))
    o_cmp = _dot(pn.astype(BF16), ckv[:, LANES:])
    psum = jnp.concatenate(
        [sum(pn[(k * GQA + g) * T8:(k * GQA + g + 1) * T8] for g in range(GQA)) for k in range(N_KV_HEADS)], axis=0)
    hi, lw = _split_bf16(psum)
    imp = _dot(hi, ov_ref[...]) + _dot(lw, ov_ref[...])
    n_blk = past // SEL_BLOCK
    lane = lax.broadcasted_iota(jnp.int32, (N_KV_HEADS * T8, LANES), 1)
    forced = (lane == 0) | (lane == n_blk - 1)
    score = jnp.where(forced, BIG, jnp.where(lane < n_blk, imp, -BIG))
    cnt = jnp.zeros(score.shape, F32)
    for jp in range(n_blk):
        r = score[:, jp:jp + 1]
        cnt = cnt + _ones_where(r > score) + jnp.where(r == score, _ones_where(lane > jp), 0.0)
    bias = jnp.where(cnt < float(N_SEL - 1), 0.0, NEG).astype(BF16)
    selbias = jnp.concatenate([bias[k * T8:(k + 1) * T8] for k in range(N_KV_HEADS) for _ in range(GQA)], axis=0)
    kt = buf_ref[slot, 0].astype(BF16)
    vt = buf_ref[slot, 1].astype(BF16)
    q_aug = jnp.concatenate([q, selbias], axis=1)
    k_aug = jnp.concatenate([kt, e_ref[...]], axis=0)
    o_slc = attend(kt, vt, tail(new[:, 2 * KV_W:3 * KV_W]), tail(new[:, 3 * KV_W:4 * KV_W]), (q_aug, k_aug), past)
    nw = win_ref.shape[3]
    o_win = attend(win_ref[0, 0].astype(BF16), win_ref[0, 1].astype(BF16),
                   tail(new[:, 4 * KV_W:5 * KV_W]), tail(new[:, 5 * KV_W:6 * KV_W]), None, nw)
    gates = gate_ref[0]
    o = gates[:, 0:1] * o_cmp + gates[:, 1:2] * o_slc + gates[:, 2:3] * o_win
    o_r = pltpu.roll(o, HALF, axis=1)
    lo = lax.broadcasted_iota(jnp.int32, (T8, LANES), 1) < HALF
    groups = []
    for k in range(N_KV_HEADS):
        for i in range(GQA // 2):
            r_even = slice((k * GQA + 2 * i) * T8, (k * GQA + 2 * i + 1) * T8)
            r_odd = slice((k * GQA + 2 * i + 1) * T8, (k * GQA + 2 * i + 2) * T8)
            groups.append(jnp.where(lo, o[r_even], o_r[r_odd]) if k == 0 else jnp.where(lo, o_r[r_even], o[r_odd]))
    o_ref[0] = _rms_norm(jnp.concatenate(groups, axis=1), go_ref[...]).astype(BF16)


def _nsa_sample(page_table, cache_t, q_bd, gate_rows, ckv, win_t, new8, e_mat, ov, go, past, n_new):
    nseq, n_pages = page_table.shape
    assert past == n_pages * PAGE_SIZE and past % SEL_BLOCK == 0 and past // SEL_BLOCK <= LANES
    assert n_new <= T8 and T8 % n_new == 0 and past % CMP_STRIDE == 0
    per_seq = lambda a: pl.BlockSpec((1,) + a.shape[1:], lambda i, pt: (i,) + (0,) * (a.ndim - 1))
    cs = lambda a: pl.BlockSpec(a.shape, lambda i, pt: (0,) * a.ndim, pipeline_mode=pl.Buffered(1))
    return pl.pallas_call(
        functools.partial(_nsa_sample_body, n_pages=n_pages, past=past, n_new=n_new,
                          n_cmp=past // CMP_STRIDE - 1),
        grid_spec=pltpu.PrefetchScalarGridSpec(
            num_scalar_prefetch=1, grid=(nseq,),
            in_specs=[pl.BlockSpec(memory_space=pl.ANY), per_seq(q_bd), per_seq(gate_rows), per_seq(ckv),
                      per_seq(win_t), per_seq(new8), cs(e_mat), cs(ov), cs(go)],
            out_specs=pl.BlockSpec((1, T8, ATTN_W), lambda i, pt: (i, 0, 0)),
            scratch_shapes=[pltpu.VMEM((2, 2, KV_W, past), F32), pltpu.SemaphoreType.DMA((2,))]),
        out_shape=jax.ShapeDtypeStruct((nseq, T8, ATTN_W), BF16),
        compiler_params=_params(1), name="nsa_sample",
    )(page_table, cache_t, q_bd, gate_rows, ckv, win_t, new8, e_mat, ov, go)


def _overlap(n_rows, n_cmp, lane0):
    n = np.arange(n_rows)[:, None]
    j = np.arange(LANES)[None, :] - lane0
    c_end = n * CMP_STRIDE + (CMP_LEN - 1)
    ov = (j * SEL_BLOCK <= c_end) & (j * SEL_BLOCK + SEL_BLOCK > c_end + 1 - CMP_LEN) & (n < n_cmp) & (j >= 0)
    return jnp.asarray(ov, BF16)


def _compress_weights(k_pos, k_w1, k_b1, k_w2, v_pos, v_w1, v_b1, v_w2):
    eye = jnp.eye(4, dtype=F32)
    feat = 4 * HEAD_DIM
    w1 = jnp.stack([k_w1, k_w1, v_w1, v_w1])
    half = lambda a: jnp.einsum("cldh,ce->lcdeh", a, eye).reshape(CMP_STRIDE, feat, 4 * CMP_HIDDEN)
    wab = jnp.concatenate([half(w1[:, :CMP_STRIDE]), half(w1[:, CMP_STRIDE:])], axis=2).astype(BF16)
    pos = jnp.stack([k_pos, k_pos, v_pos, v_pos])
    flat = lambda a: jnp.broadcast_to(jnp.transpose(a, (1, 0, 2)).reshape(CMP_STRIDE, 1, feat), (CMP_STRIDE, 8, feat))
    pos16 = jnp.concatenate([flat(pos[:, :CMP_STRIDE]), flat(pos[:, CMP_STRIDE:])], axis=1).astype(BF16)
    b1 = jnp.concatenate([k_b1, k_b1, v_b1, v_b1]).reshape(1, -1)
    z = jnp.zeros((CMP_HIDDEN, HEAD_DIM), F32)
    w2p = jnp.concatenate([
        jnp.concatenate([k_w2, z, z, z, z, z, z, z], axis=1), jnp.concatenate([z, z, k_w2, z, z, z, z, z], axis=1),
        jnp.concatenate([z, z, z, z, v_w2, z, z, z], axis=1), jnp.concatenate([z, z, z, z, z, z, v_w2, z], axis=1),
    ], axis=0).astype(BF16)
    w2s = jnp.concatenate([
        jnp.concatenate([k_w2, z, z, z], axis=1), jnp.concatenate([z, k_w2, z, z], axis=1),
        jnp.concatenate([z, z, v_w2, z], axis=1), jnp.concatenate([z, z, z, v_w2], axis=1)], axis=0).astype(BF16)
    return wab, pos16, b1, w2p, w2s


def _proj_weight(w_in):
    d = w_in.shape[0]
    g0 = ATTN_W + 6 * KV_W
    wg = jnp.concatenate([w_in[:, g0:g0 + 3 * N_HEADS], jnp.zeros((d, LANES - 3 * N_HEADS), F32)], axis=1)
    return jnp.concatenate([w_in[:, :ATTN_W] * HEAD_DIM ** -0.5, w_in[:, ATTN_W:g0], wg,
                            w_in[:, g0 + 3 * N_HEADS:]], axis=1).astype(BF16)


def _feature_major(a):
    lead = a.shape[:-4]
    n = len(lead)
    a = jnp.transpose(a, tuple(range(n)) + (n + 1, n + 2, n + 3, n))
    return a.reshape(lead + (2, KV_W, a.shape[-1]))


def _token_major(a, nb):
    return jnp.transpose(a.reshape(nb, 2, N_KV_HEADS, HEAD_DIM, a.shape[-1]), (0, 4, 1, 2, 3))


def kernel(x_prompt, x_sample, cache_cmp_kv, cache_slc_kv, state_win_kv, page_table, ln1_g, ln1_b, ffn1_w_in, ffn1_w_out, w_in, cmp_k_pos, cmp_k_w1, cmp_k_b1, cmp_k_w2, cmp_v_pos, cmp_v_w1, cmp_v_b1, cmp_v_w2, sgu_ln_g, sgu_ln_b, sgu_w, sgu_b, out_norm_g, w_out, ln2_g, ln2_b, ffn2_w_in, ffn2_w_out, ln3_g, ln3_b):
    depth = w_in.shape[0]
    nb, seq, d = x_prompt.shape
    ns, n_new, _ = x_sample.shape
    n_pages = page_table.shape[1]
    past = n_pages * PAGE_SIZE
    d_ff = ffn1_w_out.shape[1]
    alpha = (2.0 * depth) ** 0.25
    row = lambda a: a.reshape(1, -1)
    hp = x_prompt.reshape(nb * seq, d)
    hs = x_sample.reshape(ns * n_new, d)
    kv5 = (N_KV_HEADS, HEAD_DIM)
    outs = [[] for _ in range(7)]
    ov_p = _overlap(seq // CMP_STRIDE, seq // CMP_STRIDE - 1, HALF)
    ov_s = _overlap(past // CMP_STRIDE, past // CMP_STRIDE - 1, 0)
    e_mat = jnp.asarray(np.arange(LANES)[:, None] == np.arange(past)[None, :] // SEL_BLOCK, BF16)
    for l in range(depth):
        f1 = (ffn1_w_in[l, :, :d_ff].astype(BF16), ffn1_w_in[l, :, d_ff:].astype(BF16), ffn1_w_out[l].astype(BF16),
              row(ln1_g[l]), row(ln1_b[l]))
        f2 = (ffn2_w_in[l, :, :d_ff].astype(BF16), ffn2_w_in[l, :, d_ff:].astype(BF16), ffn2_w_out[l].astype(BF16),
              row(ln3_g[l]), row(ln3_b[l]))
        wp = _proj_weight(w_in[l])
        wab, pos16, b1, w2p, w2s = _compress_weights(cmp_k_pos[l], cmp_k_w1[l], cmp_k_b1[l], cmp_k_w2[l],
                                                     cmp_v_pos[l], cmp_v_w1[l], cmp_v_b1[l], cmp_v_w2[l])
        sg, sb = row(sgu_ln_g[l]), row(sgu_ln_b[l])
        go_a, go_g = row(out_norm_g[l, :ATTN_W]), row(out_norm_g[l, ATTN_W:])
        wm = w_out[l].astype(BF16)
        mix_p = jnp.tril(sgu_w[l]).astype(BF16)
        mb_p = jnp.repeat(sgu_b[l].T, GMLP_DIM, axis=1)
        r = min(n_new, CHUNK)
        mix_s = jax.vmap(lambda w: jnp.kron(jnp.eye(CHUNK // r, dtype=F32), w))(jnp.tril(sgu_w[l, :, :r, :r])).astype(BF16)
        mb_s = jnp.tile(jnp.repeat(sgu_b[l, :, :r].T, GMLP_DIM, axis=1), (CHUNK // r, 1))

        hp = _ffn_ln(hp, *f1, alpha)
        (qpad, cmpk_p, cmpv_p, _, _, cmp_t, slc_t, win_t, ksa, vsd, kwp, vwd, gate, ogn, _) = _proj(
            hp, wp, sg, sb, mix_p, mb_p, go_g, seq)
        ckv = _compress_prompt(cmpk_p, cmpv_p, nb, wab, pos16, b1, w2p)
        oan = _nsa_prompt(qpad, gate, ckv, ksa, vsd, kwp, vwd, ov_p, go_a, nb, seq)
        hp = _ffn_ln(hp, *f2, alpha, merge_args=(oan, ogn, wm, row(ln2_g[l]), row(ln2_b[l])))
        outs[0].append(_token_major(cmp_t, nb))
        outs[2].append(_token_major(slc_t, nb))
        outs[4].append(_token_major(win_t[:, :, seq - min(WINDOW, seq):], nb))

        hs = _ffn_ln(hs, *f1, alpha)
        (qpad, cmpk_s, cmpv_s, slc_s, win_s, _, _, _, _, _, _, _, gate, ogn, v_s) = _proj(
            hs, wp, sg, sb, mix_s, mb_s, go_g, ns * n_new)
        cmp_s = jnp.concatenate([cmpk_s, cmpv_s], axis=1)
        ckv = _compress_sample(page_table, _feature_major(cache_cmp_kv[l]), wab, pos16, b1, w2s)
        q5 = qpad.reshape(ns, n_new, N_KV_HEADS, GQA, LANES)[..., :HEAD_DIM]
        q5 = jnp.pad(jnp.transpose(q5, (0, 2, 3, 1, 4)), ((0, 0), (0, 0), (0, 0), (0, T8 - n_new), (0, 0)))
        zq = jnp.zeros_like(q5[:, 0])
        q_bd = jnp.stack([jnp.concatenate([q5[:, 0], zq], axis=-1), jnp.concatenate([zq, q5[:, 1]], axis=-1)],
                         axis=1).reshape(ns, N_KV_HEADS * GQA * T8, LANES)
        g5 = gate[:, :3 * N_HEADS].reshape(ns, n_new, 3, N_KV_HEADS, GQA)
        g5 = jnp.pad(jnp.transpose(g5, (0, 3, 4, 1, 2)), ((0, 0), (0, 0), (0, 0), (0, T8 - n_new), (0, 0)))
        gate_rows = g5.reshape(ns, N_KV_HEADS * GQA * T8, 3)
        new8 = jnp.pad(jnp.concatenate([cmp_s, slc_s, win_s], axis=1).reshape(ns, n_new, 6 * KV_W),
                       ((0, 0), (0, T8 - n_new), (0, 0)))
        oan = _nsa_sample(page_table, _feature_major(cache_slc_kv[l]), q_bd, gate_rows, ckv,
                          _feature_major(state_win_kv[l]), new8, e_mat, ov_s, go_a, past, n_new)
        oan = oan[:, :n_new].reshape(ns * n_new, ATTN_W)
        hs = _ffn_ln(hs, *f2, alpha, merge_args=(oan, ogn, wm, row(ln2_g[l]), row(ln2_b[l])))
        win_all = jnp.concatenate([state_win_kv[l], win_s.reshape(ns, n_new, 2, *kv5)], axis=1)
        outs[1].append(cmp_s.reshape(ns, n_new, 2, *kv5))
        outs[3].append(slc_s.reshape(ns, n_new, 2, *kv5))
        outs[5].append(win_all[:, -min(WINDOW, past + n_new):])
        outs[6].append(v_s.reshape(ns, n_new, N_GMLP, GMLP_DIM))
    st = [jnp.stack(o, axis=0) for o in outs]
    return (hp.reshape(nb, seq, d), hs.reshape(ns, n_new, d), st[0], st[1], st[2], st[3], st[4], st[5], st[6])
```

```python
import functools

import numpy as np
import jax
import jax.numpy as jnp
from jax import lax
from jax.experimental import pallas as pl
from jax.experimental.pallas import tpu as pltpu

HEAD_DIM = 64
N_HEADS = 8
N_KV_HEADS = 2
GQA = N_HEADS // N_KV_HEADS
ATTN_W = N_HEADS * HEAD_DIM
KV_W = N_KV_HEADS * HEAD_DIM
CMP_LEN = 32
CMP_STRIDE = 16
CMP_HIDDEN = 64
SEL_BLOCK = 64
N_SEL = 16
WINDOW = 512
N_GMLP = 8
GMLP_DIM = 64
GMLP_W = N_GMLP * GMLP_DIM
CHUNK = 128
PAGE_SIZE = 128
LN_EPS = 1e-5
NEG = -1e30
BIG = 1e9

LANES = 128
HALF = LANES // 2
VMEM_LIMIT = 56 * 1024 * 1024
SLOPES = [2.0 ** -(h + 1) for h in range(N_HEADS)]

F32 = jnp.float32
BF16 = jnp.bfloat16

_Q0, _KV0, _G0, _U0, _V0, _PW = 0, ATTN_W, ATTN_W + 6 * KV_W, ATTN_W + 6 * KV_W + LANES, \
    ATTN_W + 6 * KV_W + LANES + GMLP_W, ATTN_W + 6 * KV_W + LANES + 2 * GMLP_W


def _dot(a, b):
    return jnp.dot(a, b, preferred_element_type=F32)


def _dot_nt(a, b):
    return lax.dot_general(a, b, (((1,), (1,)), ((), ())), preferred_element_type=F32)


def _layer_norm(y, g, b):
    mu = jnp.mean(y, axis=-1, keepdims=True)
    d = y - mu
    var = jnp.mean(d * d, axis=-1, keepdims=True)
    return d * lax.rsqrt(var + LN_EPS) * g + b


def _rms_norm(y, g):
    return y * lax.rsqrt(jnp.mean(y * y, axis=-1, keepdims=True) + LN_EPS) * g


def _ones_where(mask):
    return jnp.where(mask, 1.0, 0.0)


def _alibi_lanes(lane, hi, lo):
    return jnp.where(lane == HALF, hi, jnp.where(lane == HALF + 1, lo, 0.0))


def _log2(n):
    assert n > 0 and n & (n - 1) == 0, n
    return n.bit_length() - 1


def _const_spec(shape):
    nd = len(shape)
    return pl.BlockSpec(shape, lambda *_: (0,) * nd, pipeline_mode=pl.Buffered(1))


def _params(n_grid):
    return pltpu.CompilerParams(dimension_semantics=("arbitrary",) * n_grid, vmem_limit_bytes=VMEM_LIMIT)


def _ffn_body(*refs, alpha, merge):
    if merge:
        (h_ref, oa_ref, og_ref, wm_ref, gm_ref, bm_ref,
         wa_ref, wb_ref, wo_ref, g_ref, b_ref, o_ref) = refs
        cat = jnp.concatenate([oa_ref[...], og_ref[...]], axis=1)
        x = _layer_norm(alpha * h_ref[...] + _dot(cat, wm_ref[...]), gm_ref[...], bm_ref[...])
    else:
        x_ref, wa_ref, wb_ref, wo_ref, g_ref, b_ref, o_ref = refs
        x = x_ref[...]
    xb = x.astype(BF16)
    a = _dot(xb, wa_ref[...])
    b = _dot(xb, wb_ref[...])
    hid = ((a * jax.nn.sigmoid(a)) * b).astype(BF16)
    y = alpha * x + 0.5 * _dot(hid, wo_ref[...])
    o_ref[...] = _layer_norm(y, g_ref[...], b_ref[...])


def _ffn_ln(x, wa, wb, wo, g, b, alpha, merge_args=None, tm=512):
    t, d = x.shape
    tm = min(tm, t)
    assert t % tm == 0
    row = lambda w: pl.BlockSpec((tm, w), lambda i: (i, 0))
    ins, specs = [x], [row(d)]
    if merge_args is not None:
        oa, og, wm, gm, bm = merge_args
        ins += [oa, og, wm, gm, bm]
        specs += [row(oa.shape[1]), row(og.shape[1]), _const_spec(wm.shape), _const_spec(gm.shape),
                  _const_spec(bm.shape)]
    ins += [wa, wb, wo, g, b]
    specs += [_const_spec(a.shape) for a in (wa, wb, wo, g, b)]
    return pl.pallas_call(
        functools.partial(_ffn_body, alpha=alpha, merge=merge_args is not None),
        grid=(t // tm,), in_specs=specs, out_specs=row(d),
        out_shape=jax.ShapeDtypeStruct((t, d), F32), compiler_params=_params(1),
        name="ffn_merge_ln" if merge_args is not None else "ffn_ln",
    )(*ins)


def _proj_body(h_ref, w_ref, sg_ref, sb_ref, mix_ref, mb_ref, go_ref,
               qpad_ref, cmpk_ref, cmpv_ref, slc_ref, win_ref, cmpt_ref, slct_ref, wint_ref,
               ksa_ref, vsd_ref, kwp_ref, vwd_ref, gate_ref, ogn_ref, v_ref, *, tm, seq):
    p = _dot(h_ref[...].astype(BF16), w_ref[...])
    lane = lax.broadcasted_iota(jnp.int32, (tm, LANES), 1)
    lo = lane < HALF
    tok = lax.rem(pl.program_id(0) * tm, seq) + lax.broadcasted_iota(jnp.int32, (tm, LANES), 0)
    for i in range(ATTN_W // LANES):
        x = p[:, _Q0 + i * LANES:_Q0 + (i + 1) * LANES]
        xr = pltpu.roll(x, HALF, axis=1)
        qpad_ref[:, (2 * i) * LANES:(2 * i + 1) * LANES] = jnp.where(
            lo, x, _alibi_lanes(lane, LANES * SLOPES[2 * i], SLOPES[2 * i])).astype(BF16)
        qpad_ref[:, (2 * i + 1) * LANES:(2 * i + 2) * LANES] = jnp.where(
            lo, xr, _alibi_lanes(lane, LANES * SLOPES[2 * i + 1], SLOPES[2 * i + 1])).astype(BF16)
    for s, feat_ref in enumerate((cmpt_ref, slct_ref, wint_ref)):
        feat_ref[0] = p[:, _KV0 + 2 * s * KV_W:_KV0 + 2 * (s + 1) * KV_W].T
    cmpk_ref[...] = p[:, _KV0:_KV0 + KV_W]
    cmpv_ref[...] = p[:, _KV0 + KV_W:_KV0 + 2 * KV_W]
    slc_ref[...] = p[:, _KV0 + 2 * KV_W:_KV0 + 4 * KV_W]
    win_ref[...] = p[:, _KV0 + 4 * KV_W:_KV0 + 6 * KV_W]
    onehot = _ones_where(lane - HALF == (tok >> _log2(SEL_BLOCK)))
    ks = p[:, _KV0 + 2 * KV_W:_KV0 + 3 * KV_W]
    ksr = pltpu.roll(ks, HALF, axis=1)
    ksa_ref[:, :LANES] = jnp.where(lo, ks, onehot).astype(BF16)
    ksa_ref[:, LANES:] = jnp.where(lo, ksr, onehot).astype(BF16)
    vs = p[:, _KV0 + 3 * KV_W:_KV0 + 4 * KV_W]
    vsr = pltpu.roll(vs, HALF, axis=1)
    vsd_ref[:, :LANES] = jnp.where(lo, vs, 1.0).astype(BF16)
    vsd_ref[:, LANES:] = jnp.where(lo, vsr, 1.0).astype(BF16)
    kw = p[:, _KV0 + 4 * KV_W:_KV0 + 5 * KV_W]
    kwr = pltpu.roll(kw, HALF, axis=1)
    pos_pair = _alibi_lanes(lane, (tok >> _log2(LANES)).astype(F32), (tok & (LANES - 1)).astype(F32))
    kwp_ref[:, :LANES] = jnp.where(lo, kw, pos_pair).astype(BF16)
    kwp_ref[:, LANES:] = jnp.where(lo, kwr, pos_pair).astype(BF16)
    vw = p[:, _KV0 + 5 * KV_W:_KV0 + 6 * KV_W]
    vwr = pltpu.roll(vw, HALF, axis=1)
    vwd_ref[:, :LANES] = jnp.where(lo, vw, 1.0).astype(BF16)
    vwd_ref[:, LANES:] = jnp.where(lo, vwr, 1.0).astype(BF16)
    gate_ref[...] = jax.nn.sigmoid(p[:, _G0:_G0 + LANES])
    u = jax.nn.gelu(p[:, _U0:_U0 + GMLP_W])
    v = _layer_norm(jax.nn.gelu(p[:, _V0:_V0 + GMLP_W]), sg_ref[...], sb_ref[...])
    v_ref[...] = v
    vb = v.astype(BF16)
    lo_c = lax.broadcasted_iota(jnp.int32, (CHUNK, LANES), 1) < HALF
    for c in range(tm // CHUNK):
        rows = slice(c * CHUNK, (c + 1) * CHUNK)
        zs = []
        for i in range(GMLP_W // LANES):
            vp = vb[rows, i * LANES:(i + 1) * LANES]
            zs.append(jnp.where(lo_c, _dot(mix_ref[2 * i], vp), _dot(mix_ref[2 * i + 1], vp)))
        z = jnp.concatenate(zs, axis=1) + mb_ref[...]
        ogn_ref[rows, :] = _rms_norm(u[rows, :] * z, go_ref[...]).astype(BF16)


def _proj(h, w, sg, sb, mix, mb, go, seq, tm=1024):
    t, d = h.shape
    tm = min(tm, t)
    assert t % tm == 0 and tm % CHUNK == 0 and seq % tm == 0 and t % seq == 0 and seq // SEL_BLOCK <= HALF
    nq = seq // tm
    row = lambda width: pl.BlockSpec((tm, width), lambda i: (i, 0))
    feat = pl.BlockSpec((1, 2 * KV_W, tm), lambda i: (i // nq, 0, i % nq))
    tok_major = [(2 * ATTN_W, BF16), (KV_W, F32), (KV_W, F32), (2 * KV_W, F32), (2 * KV_W, F32)]
    rest = [(2 * LANES, BF16), (2 * LANES, BF16), (2 * LANES, BF16), (2 * LANES, BF16), (LANES, F32),
            (GMLP_W, BF16), (GMLP_W, F32)]
    sds = jax.ShapeDtypeStruct
    return pl.pallas_call(
        functools.partial(_proj_body, tm=tm, seq=seq),
        grid=(t // tm,),
        in_specs=[row(d)] + [_const_spec(a.shape) for a in (w, sg, sb, mix, mb, go)],
        out_specs=[row(wd) for wd, _ in tok_major] + [feat] * 3 + [row(wd) for wd, _ in rest],
        out_shape=[sds((t, wd), dt) for wd, dt in tok_major] + [sds((t // seq, 2 * KV_W, seq), F32)] * 3
        + [sds((t, wd), dt) for wd, dt in rest],
        compiler_params=_params(1), name="proj",
    )(h, w, sg, sb, mix, mb, go)


def _compress_math(rows_at, nh, wab_ref, pos_ref, b1_ref, w2_ref):
    hw = b1_ref.shape[1]
    ab = None
    for l in range(CMP_STRIDE):
        lhs = jnp.concatenate([rows_at(l).astype(BF16), pos_ref[l]], axis=0)
        d = _dot(lhs, wab_ref[l])
        ab = d if ab is None else ab + d
    c = ab[nh:nh + 1, :hw] + ab[nh + 8:nh + 9, hw:] + b1_ref[...]
    hid = ab[:nh, :hw] + pltpu.roll(ab[:nh, hw:], nh - 1, axis=0) + c
    return _dot(jax.nn.gelu(hid).astype(BF16), w2_ref[...])


def _strided_rows(k_ref, v_ref, nh):
    return lambda l: jnp.concatenate([k_ref[pl.ds(l, nh, stride=CMP_STRIDE), :],
                                      v_ref[pl.ds(l, nh, stride=CMP_STRIDE), :]], axis=1)


def _compress_body(xk_ref, xv_ref, wab_ref, pos_ref, b1_ref, w2_ref, o_ref):
    nh = o_ref.shape[1]
    out = _compress_math(_strided_rows(xk_ref, xv_ref, nh), nh, wab_ref, pos_ref, b1_ref, w2_ref)
    lane = lax.broadcasted_iota(jnp.int32, (nh, LANES), 1)
    c_end = lax.broadcasted_iota(jnp.int32, (nh, LANES), 0) * CMP_STRIDE + (CMP_LEN - 1)
    pair = _alibi_lanes(lane, (c_end >> _log2(LANES)).astype(F32), (c_end & (LANES - 1)).astype(F32))
    for k in range(N_KV_HEADS):
        o_ref[0, :, k * LANES:(k + 1) * LANES] = (out[:, k * LANES:(k + 1) * LANES] + pair).astype(o_ref.dtype)
    o_ref[0, :, N_KV_HEADS * LANES:] = out[:, N_KV_HEADS * LANES:].astype(o_ref.dtype)


def _compress_prompt(xk, xv, nb, wab, pos, b1, w2):
    seq = xk.shape[0] // nb
    nh = seq // CMP_STRIDE
    return pl.pallas_call(
        _compress_body, grid=(nb,),
        in_specs=[pl.BlockSpec((seq, KV_W), lambda i: (i, 0))] * 2 + [_const_spec(a.shape) for a in (wab, pos, b1, w2)],
        out_specs=pl.BlockSpec((1, nh, w2.shape[1]), lambda i: (i, 0, 0)),
        out_shape=jax.ShapeDtypeStruct((nb, nh, w2.shape[1]), BF16),
        compiler_params=_params(1), name="compress_prompt",
    )(xk, xv, wab, pos, b1, w2)


def _gather_step(make_copies):
    i, n = pl.program_id(0), pl.num_programs(0)
    slot = lax.rem(i, 2)

    @pl.when(i == 0)
    def _():
        for cp in make_copies(i, slot):
            cp.start()

    @pl.when(i + 1 < n)
    def _():
        for cp in make_copies(i + 1, 1 - slot):
            cp.start()

    for cp in make_copies(i, slot):
        cp.wait()
    return slot


def _compress_sample_body(pt_ref, cache_ref, perm_ref, wab_ref, pos_ref, b1_ref, w2_ref, o_ref, buf_ref, xs_ref,
                          sem_ref, *, n_pages):
    def make_copies(seq_idx, slot):
        return [pltpu.make_async_copy(cache_ref.at[pt_ref[seq_idx, j]], buf_ref.at[slot, j], sem_ref.at[slot])
                for j in range(n_pages)]

    slot = _gather_step(make_copies)
    per_page = PAGE_SIZE // CMP_STRIDE

    feat, page = buf_ref.shape[3], buf_ref.shape[4]
    xt = buf_ref[slot].reshape(n_pages * 2 * feat, page)
    z = _dot_nt(perm_ref[...], xt.astype(BF16))
    for j in range(n_pages):
        for c in range(2):
            col = slice((2 * j + c) * feat, (2 * j + c + 1) * feat)
            for l in range(CMP_STRIDE):
                xs_ref[c, l, j * per_page:(j + 1) * per_page, :] = z[l * per_page:(l + 1) * per_page, col]
    nh = o_ref.shape[1]
    rows_at = lambda l: jnp.concatenate([xs_ref[0, l], xs_ref[1, l]], axis=1)
    o_ref[0] = _compress_math(rows_at, nh, wab_ref, pos_ref, b1_ref, w2_ref).astype(o_ref.dtype)


def _compress_sample(page_table, cache_t, wab, pos, b1, w2):
    nseq, n_pages = page_table.shape
    _, _, feat, page = cache_t.shape
    nh = n_pages * page // CMP_STRIDE
    cs = lambda a: pl.BlockSpec(a.shape, lambda i, pt: (0,) * a.ndim, pipeline_mode=pl.Buffered(1))
    tok = np.arange(page)
    perm = jnp.asarray((tok % CMP_STRIDE * (page // CMP_STRIDE) + tok // CMP_STRIDE)[None, :] == tok[:, None], BF16)
    return pl.pallas_call(
        functools.partial(_compress_sample_body, n_pages=n_pages),
        grid_spec=pltpu.PrefetchScalarGridSpec(
            num_scalar_prefetch=1, grid=(nseq,),
            in_specs=[pl.BlockSpec(memory_space=pl.ANY)] + [cs(a) for a in (perm, wab, pos, b1, w2)],
            out_specs=pl.BlockSpec((1, nh, w2.shape[1]), lambda i, pt: (i, 0, 0)),
            scratch_shapes=[pltpu.VMEM((2, n_pages, 2, feat, page), F32), pltpu.VMEM((2, CMP_STRIDE, nh, feat), F32),
                            pltpu.SemaphoreType.DMA((2,))]),
        out_shape=jax.ShapeDtypeStruct((nseq, nh, w2.shape[1]), BF16),
        compiler_params=_params(1), name="compress_sample",
    )(page_table, cache_t, perm, wab, pos, b1, w2)


def _masked_softmax(s, mask):
    sm = jnp.where(mask, s, NEG)
    m = jnp.max(sm, axis=-1, keepdims=True)
    p = jnp.where(mask, jnp.exp(sm - m), 0.0)
    l = jnp.sum(p, axis=-1, keepdims=True)
    return p * (1.0 / jnp.maximum(l, 1e-30))


def _split_bf16(x):
    hi = x.astype(BF16)
    return hi, (x - hi.astype(F32)).astype(BF16)


def _alibi_slopes(head_idx):
    out = jnp.zeros(head_idx.shape, F32)
    for h in range(N_HEADS):
        out = jnp.where(head_idx == h, SLOPES[h], out)
    return out


def _nsa_prompt_body(q_ref, gate_ref, ckv_ref, ksa_ref, vsd_ref, kwp_ref, vwd_ref, ov_ref, go_ref,
                     o_ref, s_ref, qsel_ref, mrun_ref, mb_ref, acc_ref, tile_ref, *, tq, tk, nwin):
    n_tiles = s_ref.shape[1]
    t0 = pl.program_id(1) * tq
    rows = GQA * tq
    lane = lax.broadcasted_iota(jnp.int32, (tq, LANES), 1)
    lo = lane < HALF
    t_rel = lax.broadcasted_iota(jnp.int32, (rows, 1), 0) & (tq - 1)
    cur = (t0 + lax.broadcasted_iota(jnp.int32, (tq, 1), 0)) >> _log2(SEL_BLOCK)
    sub = lax.broadcasted_iota(jnp.int32, (8, tq), 0)
    gates = gate_ref[...]
    nh = ckv_ref.shape[1]
    c_rel = lax.broadcasted_iota(jnp.int32, (1, nh), 1) * CMP_STRIDE + (CMP_LEN - 1) - t0
    n_full = t0 // tk
    w_start = pl.multiple_of(jnp.maximum(t0 - WINDOW, 0), tq)
    w_rel = (w_start - t0) + lax.broadcasted_iota(jnp.int32, (1, nwin), 1)
    t_q = lax.broadcasted_iota(jnp.int32, (tq, 1), 0)
    w_bias = jnp.where((w_rel <= t_q) & (w_rel >= t_q - WINDOW), 0.0, NEG)
    cols = [slice(k * LANES, (k + 1) * LANES) for k in range(N_KV_HEADS)]
    blk_tile = lax.broadcasted_iota(jnp.int32, (HALF, 1), 0) >> _log2(tk // SEL_BLOCK)

    def add_alibi(s, kvh, rel):
        relf = rel.astype(F32)
        return jnp.concatenate([s[g * tq:(g + 1) * tq] + SLOPES[kvh * GQA + g] * relf for g in range(GQA)], axis=0)

    o_cmp, o_win = [], []
    tile_hits = [jnp.int32(0)] * n_tiles
    for kvh in range(N_KV_HEADS):
        qs = [q_ref[:, (kvh * GQA + g) * LANES:(kvh * GQA + g + 1) * LANES] for g in range(GQA)]
        qall = jnp.concatenate(qs, axis=0)
        s = _dot_nt(qall, kwp_ref[pl.ds(w_start, nwin), cols[kvh]])
        s = jnp.concatenate([s[g * tq:(g + 1) * tq] + w_bias for g in range(GQA)], axis=0)
        pw = jnp.exp((s - jnp.max(s, axis=-1, keepdims=True)).astype(BF16))
        o_win.append(_dot(pw, vwd_ref[pl.ds(w_start, nwin), cols[kvh]]))
        pn = _masked_softmax(_dot_nt(qall, ckv_ref[0, :, cols[kvh]]), c_rel <= t_rel)
        o_cmp.append(_dot(pn.astype(BF16), ckv_ref[0, :, 2 * LANES + kvh * LANES:2 * LANES + (kvh + 1) * LANES]))
        psum = pn[0:tq] + pn[tq:2 * tq] + pn[2 * tq:3 * tq] + pn[3 * tq:4 * tq]
        hi, lw = _split_bf16(psum)
        imp = _dot(hi, ov_ref[...]) + _dot(lw, ov_ref[...])
        j = lane - HALF
        forced = (j == 0) | (j == cur) | (j == cur - 1)
        score = jnp.where(forced, BIG, jnp.where(j <= cur, imp, -BIG))
        sc_t = score.T[HALF:, :]
        cnt = [jnp.zeros((8, tq), F32) for _ in range(HALF // 8)]
        for jp in range(HALF):
            r = sc_t[jp:jp + 1, :]
            for v in range(HALF // 8):
                blk = sc_t[8 * v:8 * v + 8, :]
                if 8 * v > jp:
                    beats = _ones_where(r >= blk)
                elif 8 * v + 7 <= jp:
                    beats = _ones_where(r > blk)
                else:
                    beats = _ones_where(r > blk) + jnp.where(r == blk, _ones_where(sub > jp - 8 * v), 0.0)
                cnt[v] = cnt[v] + beats
        sel = jnp.concatenate(cnt, axis=0) < float(N_SEL)
        selbias = jnp.concatenate([jnp.zeros((HALF, tq), F32), jnp.where(sel, 0.0, NEG)], axis=0).T.astype(BF16)
        qsel_ref[kvh] = jnp.concatenate([jnp.where(lo, q, selbias) for q in qs], axis=0)
        any_sel = jnp.max(_ones_where(sel), axis=-1, keepdims=True)
        for w in range(-(-n_tiles // 8)):
            weight = jnp.zeros((HALF, 1), F32)
            for k in range(8):
                weight = jnp.where(blk_tile == 8 * w + k, float(8 ** k), weight)
            packed = jnp.sum(any_sel * weight).astype(jnp.int32)
            for k in range(min(8, n_tiles - 8 * w)):
                tile_hits[8 * w + k] = tile_hits[8 * w + k] + ((packed >> (3 * k)) & 7)
        mrun_ref[kvh] = jnp.full((rows, LANES), NEG, F32)
        acc_ref[kvh] = jnp.zeros((rows, LANES), F32)

    def scores(kvh, jt, causal):
        k0 = pl.multiple_of(jt * tk, tk)
        k_rel = (k0 - t0) + lax.broadcasted_iota(jnp.int32, (1, tk), 1)
        s = add_alibi(_dot_nt(qsel_ref[kvh], ksa_ref[pl.ds(k0, tk), cols[kvh]]), kvh, k_rel)
        if causal:
            s = jnp.where(k_rel <= t_rel, s, NEG)
        s_ref[kvh, jt] = s
        mt = s[:, :LANES]
        for c in range(1, tk // LANES):
            mt = jnp.maximum(mt, s[:, c * LANES:(c + 1) * LANES])
        mrun_ref[kvh] = jnp.maximum(mrun_ref[kvh], mt)

    n_hit = jnp.int32(0)
    for jt in range(n_tiles):
        tile_ref[n_hit] = jt
        n_hit = n_hit + ((tile_hits[jt] > 0) & (jt < n_full)).astype(jnp.int32)

    def sweep(visit):
        def pair(i, carry):
            for u in range(2):
                for kvh in range(N_KV_HEADS):
                    visit(kvh, tile_ref[2 * i + u])
            return carry

        lax.fori_loop(0, n_hit >> 1, pair, 0)

        @pl.when((n_hit & 1) == 1)
        def _():
            for kvh in range(N_KV_HEADS):
                visit(kvh, tile_ref[n_hit - 1])

    sweep(lambda kvh, jt: scores(kvh, jt, False))
    for kvh in range(N_KV_HEADS):
        scores(kvh, n_full, True)
        mb_ref[kvh] = jnp.broadcast_to(jnp.max(mrun_ref[kvh], axis=-1, keepdims=True), (rows, tk))

    def weigh(kvh, jt):
        k0 = pl.multiple_of(jt * tk, tk)
        p = jnp.exp((s_ref[kvh, jt] - mb_ref[kvh]).astype(BF16))
        acc_ref[kvh] = acc_ref[kvh] + _dot(p, vsd_ref[pl.ds(k0, tk), cols[kvh]])

    sweep(weigh)
    for kvh in range(N_KV_HEADS):
        weigh(kvh, n_full)
    head_out = []
    for kvh in range(N_KV_HEADS):
        gate_col = [jnp.concatenate([gates[:, b * N_HEADS + kvh * GQA + g:b * N_HEADS + kvh * GQA + g + 1]
                                     for g in range(GQA)], axis=0) for b in range(3)]
        pv_slc, pv_win = acc_ref[kvh], o_win[kvh]
        out = (gate_col[0] * o_cmp[kvh]
               + (gate_col[1] * (1.0 / pv_slc[:, HALF:HALF + 1])) * pv_slc
               + (gate_col[2] * (1.0 / pv_win[:, HALF:HALF + 1])) * pv_win)
        head_out += [out[g * tq:(g + 1) * tq] for g in range(GQA)]
    o = jnp.concatenate([jnp.where(lo, head_out[2 * i], pltpu.roll(head_out[2 * i + 1], HALF, axis=1))
                         for i in range(N_HEADS // 2)], axis=1)
    o_ref[...] = _rms_norm(o, go_ref[...]).astype(BF16)


def _nsa_prompt(qpad, gate, ckv, ksa, vsd, kwp, vwd, ov, go, nb, seq, tq=256, tk=256):
    assert seq % tk == 0 and tk % tq == 0 and tq % LANES == 0 and seq >= WINDOW + tq and tk // SEL_BLOCK <= 7
    nwin = WINDOW + tq
    nq = seq // tq
    qrow = lambda w: pl.BlockSpec((tq, w), lambda b, i: (b * nq + i, 0))
    seqblk = lambda w: pl.BlockSpec((seq, w), lambda b, i: (b, 0), pipeline_mode=pl.Buffered(1))
    cs = lambda a: pl.BlockSpec(a.shape, lambda b, i: (0,) * a.ndim, pipeline_mode=pl.Buffered(1))
    rows = GQA * tq
    acc = pltpu.VMEM((N_KV_HEADS, rows, LANES), F32)
    return pl.pallas_call(
        functools.partial(_nsa_prompt_body, tq=tq, tk=tk, nwin=nwin),
        grid=(nb, nq),
        in_specs=[qrow(qpad.shape[1]), qrow(LANES), pl.BlockSpec((1,) + ckv.shape[1:], lambda b, i: (b, 0, 0)),
                  seqblk(2 * LANES), seqblk(2 * LANES), seqblk(2 * LANES), seqblk(2 * LANES), cs(ov), cs(go)],
        out_specs=qrow(ATTN_W),
        out_shape=jax.ShapeDtypeStruct((nb * seq, ATTN_W), BF16),
        scratch_shapes=[pltpu.VMEM((N_KV_HEADS, seq // tk, rows, tk), F32), pltpu.VMEM((N_KV_HEADS, rows, LANES), BF16),
                        acc, pltpu.VMEM((N_KV_HEADS, rows, tk), F32), acc,
                        pltpu.SMEM((seq // tk,), jnp.int32)],
        compiler_params=_params(2), name="nsa_prompt",
    )(qpad, gate, ckv, ksa, vsd, kwp, vwd, ov, go)


T8 = 8


def _nsa_sample_body(pt_ref, cache_ref, q_ref, gate_ref, ckv_ref, win_ref, new_ref, e_ref, ov_ref, go_ref,
                     o_ref, buf_ref, sem_ref, *, n_pages, past, n_new, n_cmp):
    def make_copies(seq_idx, slot):
        return [pltpu.make_async_copy(cache_ref.at[pt_ref[seq_idx, j]],
                                      buf_ref.at[slot, :, :, pl.ds(j * PAGE_SIZE, PAGE_SIZE)], sem_ref.at[slot])
                for j in range(n_pages)]

    slot = _gather_step(make_copies)
    rows = N_KV_HEADS * GQA * T8
    ridx = lax.broadcasted_iota(jnp.int32, (rows, 1), 0)
    t_new = ridx & (n_new - 1)
    slope = _alibi_slopes(ridx >> _log2(T8))
    q = q_ref[0]
    new = new_ref[0]

    def tail(x):
        return jnp.concatenate([x, jnp.zeros((LANES - T8, x.shape[1]), F32)], axis=0).astype(BF16)

    def attend(kt, vt, knew, vnew, extra, n_past):
        rel = lax.broadcasted_iota(jnp.int32, (1, n_past + LANES), 1) - n_past
        s = jnp.concatenate([_dot(q, kt) if extra is None else _dot(*extra), _dot_nt(q, knew)], axis=1)
        s = s + slope * rel.astype(F32)
        p = _masked_softmax(s, (rel <= t_new) & (rel >= t_new - WINDOW) & (rel + past >= 0)
                            if extra is None else rel <= t_new)
        return _dot_nt(p[:, :n_past].astype(BF16), vt) + _dot(p[:, n_past:].astype(BF16), vnew)

    nh = ckv_ref.shape[1]
    n_i = lax.broadcasted_iota(jnp.int32, (1, nh), 1)
    dist = (past + t_new) - (n_i * CMP_STRIDE + (CMP_LEN - 1))
    ckv = ckv_ref[0]
    s = _dot_nt(q, ckv[:, :LANES]) - slope * dist.astype(F32)
    pn = _masked_softmax(s, (dist >= 0) & (n_i < n_cmp))
    o_cmp = _dot(pn.astype(BF16), ckv[:, LANES:])
    psum = jnp.concatenate(
        [sum(pn[(k * GQA + g) * T8:(k * GQA + g + 1) * T8] for g in range(GQA)) for k in range(N_KV_HEADS)], axis=0)
    hi, lw = _split_bf16(psum)
    imp = _dot(hi, ov_ref[...]) + _dot(lw, ov_ref[...])
    n_blk = past // SEL_BLOCK
    lane = lax.broadcasted_iota(jnp.int32, (N_KV_HEADS * T8, LANES), 1)
    forced = (lane == 0) | (lane == n_blk - 1)
    score = jnp.where(forced, BIG, jnp.where(lane < n_blk, imp, -BIG))
    cnt = jnp.zeros(score.shape, F32)
    for jp in range(n_blk):
        r = score[:, jp:jp + 1]
        cnt = cnt + _ones_where(r > score) + jnp.where(r == score, _ones_where(lane > jp), 0.0)
    bias = jnp.where(cnt < float(N_SEL - 1), 0.0, NEG).astype(BF16)
    selbias = jnp.concatenate([bias[k * T8:(k + 1) * T8] for k in range(N_KV_HEADS) for _ in range(GQA)], axis=0)
    kt = buf_ref[slot, 0].astype(BF16)
    vt = buf_ref[slot, 1].astype(BF16)
    q_aug = jnp.concatenate([q, selbias], axis=1)
    k_aug = jnp.concatenate([kt, e_ref[...]], axis=0)
    o_slc = attend(kt, vt, tail(new[:, 2 * KV_W:3 * KV_W]), tail(new[:, 3 * KV_W:4 * KV_W]), (q_aug, k_aug), past)
    nw = win_ref.shape[3]
    o_win = attend(win_ref[0, 0].astype(BF16), win_ref[0, 1].astype(BF16),
                   tail(new[:, 4 * KV_W:5 * KV_W]), tail(new[:, 5 * KV_W:6 * KV_W]), None, nw)
    gates = gate_ref[0]
    o = gates[:, 0:1] * o_cmp + gates[:, 1:2] * o_slc + gates[:, 2:3] * o_win
    o_r = pltpu.roll(o, HALF, axis=1)
    lo = lax.broadcasted_iota(jnp.int32, (T8, LANES), 1) < HALF
    groups = []
    for k in range(N_KV_HEADS):
        for i in range(GQA // 2):
            r_even = slice((k * GQA + 2 * i) * T8, (k * GQA + 2 * i + 1) * T8)
            r_odd = slice((k * GQA + 2 * i + 1) * T8, (k * GQA + 2 * i + 2) * T8)
            groups.append(jnp.where(lo, o[r_even], o_r[r_odd]) if k == 0 else jnp.where(lo, o_r[r_even], o[r_odd]))
    o_ref[0] = _rms_norm(jnp.concatenate(groups, axis=1), go_ref[...]).astype(BF16)


def _nsa_sample(page_table, cache_t, q_bd, gate_rows, ckv, win_t, new8, e_mat, ov, go, past, n_new):
    nseq, n_pages = page_table.shape
    assert past == n_pages * PAGE_SIZE and past % SEL_BLOCK == 0 and past // SEL_BLOCK <= LANES
    assert n_new <= T8 and T8 % n_new == 0 and past % CMP_STRIDE == 0
    per_seq = lambda a: pl.BlockSpec((1,) + a.shape[1:], lambda i, pt: (i,) + (0,) * (a.ndim - 1))
    cs = lambda a: pl.BlockSpec(a.shape, lambda i, pt: (0,) * a.ndim, pipeline_mode=pl.Buffered(1))
    return pl.pallas_call(
        functools.partial(_nsa_sample_body, n_pages=n_pages, past=past, n_new=n_new,
                          n_cmp=past // CMP_STRIDE - 1),
        grid_spec=pltpu.PrefetchScalarGridSpec(
            num_scalar_prefetch=1, grid=(nseq,),
            in_specs=[pl.BlockSpec(memory_space=pl.ANY), per_seq(q_bd), per_seq(gate_rows), per_seq(ckv),
                      per_seq(win_t), per_seq(new8), cs(e_mat), cs(ov), cs(go)],
            out_specs=pl.BlockSpec((1, T8, ATTN_W), lambda i, pt: (i, 0, 0)),
            scratch_shapes=[pltpu.VMEM((2, 2, KV_W, past), F32), pltpu.SemaphoreType.DMA((2,))]),
        out_shape=jax.ShapeDtypeStruct((nseq, T8, ATTN_W), BF16),
        compiler_params=_params(1), name="nsa_sample",
    )(page_table, cache_t, q_bd, gate_rows, ckv, win_t, new8, e_mat, ov, go)


def _overlap(n_rows, n_cmp, lane0):
    n = np.arange(n_rows)[:, None]
    j = np.arange(LANES)[None, :] - lane0
    c_end = n * CMP_STRIDE + (CMP_LEN - 1)
    ov = (j * SEL_BLOCK <= c_end) & (j * SEL_BLOCK + SEL_BLOCK > c_end + 1 - CMP_LEN) & (n < n_cmp) & (j >= 0)
    return jnp.asarray(ov, BF16)


def _compress_weights(k_pos, k_w1, k_b1, k_w2, v_pos, v_w1, v_b1, v_w2):
    eye = jnp.eye(4, dtype=F32)
    feat = 4 * HEAD_DIM
    w1 = jnp.stack([k_w1, k_w1, v_w1, v_w1])
    half = lambda a: jnp.einsum("cldh,ce->lcdeh", a, eye).reshape(CMP_STRIDE, feat, 4 * CMP_HIDDEN)
    wab = jnp.concatenate([half(w1[:, :CMP_STRIDE]), half(w1[:, CMP_STRIDE:])], axis=2).astype(BF16)
    pos = jnp.stack([k_pos, k_pos, v_pos, v_pos])
    flat = lambda a: jnp.broadcast_to(jnp.transpose(a, (1, 0, 2)).reshape(CMP_STRIDE, 1, feat), (CMP_STRIDE, 8, feat))
    pos16 = jnp.concatenate([flat(pos[:, :CMP_STRIDE]), flat(pos[:, CMP_STRIDE:])], axis=1).astype(BF16)
    b1 = jnp.concatenate([k_b1, k_b1, v_b1, v_b1]).reshape(1, -1)
    z = jnp.zeros((CMP_HIDDEN, HEAD_DIM), F32)
    w2p = jnp.concatenate([
        jnp.concatenate([k_w2, z, z, z, z, z, z, z], axis=1), jnp.concatenate([z, z, k_w2, z, z, z, z, z], axis=1),
        jnp.concatenate([z, z, z, z, v_w2, z, z, z], axis=1), jnp.concatenate([z, z, z, z, z, z, v_w2, z], axis=1),
    ], axis=0).astype(BF16)
    w2s = jnp.concatenate([
        jnp.concatenate([k_w2, z, z, z], axis=1), jnp.concatenate([z, k_w2, z, z], axis=1),
        jnp.concatenate([z, z, v_w2, z], axis=1), jnp.concatenate([z, z, z, v_w2], axis=1)], axis=0).astype(BF16)
    return wab, pos16, b1, w2p, w2s


def _proj_weight(w_in):
    d = w_in.shape[0]
    g0 = ATTN_W + 6 * KV_W
    wg = jnp.concatenate([w_in[:, g0:g0 + 3 * N_HEADS], jnp.zeros((d, LANES - 3 * N_HEADS), F32)], axis=1)
    return jnp.concatenate([w_in[:, :ATTN_W] * HEAD_DIM ** -0.5, w_in[:, ATTN_W:g0], wg,
                            w_in[:, g0 + 3 * N_HEADS:]], axis=1).astype(BF16)


def _feature_major(a):
    lead = a.shape[:-4]
    n = len(lead)
    a = jnp.transpose(a, tuple(range(n)) + (n + 1, n + 2, n + 3, n))
    return a.reshape(lead + (2, KV_W, a.shape[-1]))


def _token_major(a, nb):
    return jnp.transpose(a.reshape(nb, 2, N_KV_HEADS, HEAD_DIM, a.shape[-1]), (0, 4, 1, 2, 3))


def kernel(x_prompt, x_sample, cache_cmp_kv, cache_slc_kv, state_win_kv, page_table, ln1_g, ln1_b, ffn1_w_in, ffn1_w_out, w_in, cmp_k_pos, cmp_k_w1, cmp_k_b1, cmp_k_w2, cmp_v_pos, cmp_v_w1, cmp_v_b1, cmp_v_w2, sgu_ln_g, sgu_ln_b, sgu_w, sgu_b, out_norm_g, w_out, ln2_g, ln2_b, ffn2_w_in, ffn2_w_out, ln3_g, ln3_b):
    depth = w_in.shape[0]
    nb, seq, d = x_prompt.shape
    ns, n_new, _ = x_sample.shape
    n_pages = page_table.shape[1]
    past = n_pages * PAGE_SIZE
    d_ff = ffn1_w_out.shape[1]
    alpha = (2.0 * depth) ** 0.25
    row = lambda a: a.reshape(1, -1)
    hp = x_prompt.reshape(nb * seq, d)
    hs = x_sample.reshape(ns * n_new, d)
    kv5 = (N_KV_HEADS, HEAD_DIM)
    outs = [[] for _ in range(7)]
    ov_p = _overlap(seq // CMP_STRIDE, seq // CMP_STRIDE - 1, HALF)
    ov_s = _overlap(past // CMP_STRIDE, past // CMP_STRIDE - 1, 0)
    e_mat = jnp.asarray(np.arange(LANES)[:, None] == np.arange(past)[None, :] // SEL_BLOCK, BF16)
    for l in range(depth):
        f1 = (ffn1_w_in[l, :, :d_ff].astype(BF16), ffn1_w_in[l, :, d_ff:].astype(BF16), ffn1_w_out[l].astype(BF16),
              row(ln1_g[l]), row(ln1_b[l]))
        f2 = (ffn2_w_in[l, :, :d_ff].astype(BF16), ffn2_w_in[l, :, d_ff:].astype(BF16), ffn2_w_out[l].astype(BF16),
              row(ln3_g[l]), row(ln3_b[l]))
        wp = _proj_weight(w_in[l])
        wab, pos16, b1, w2p, w2s = _compress_weights(cmp_k_pos[l], cmp_k_w1[l], cmp_k_b1[l], cmp_k_w2[l],
                                                     cmp_v_pos[l], cmp_v_w1[l], cmp_v_b1[l], cmp_v_w2[l])
        sg, sb = row(sgu_ln_g[l]), row(sgu_ln_b[l])
        go_a, go_g = row(out_norm_g[l, :ATTN_W]), row(out_norm_g[l, ATTN_W:])
        wm = w_out[l].astype(BF16)
        mix_p = jnp.tril(sgu_w[l]).astype(BF16)
        mb_p = jnp.repeat(sgu_b[l].T, GMLP_DIM, axis=1)
        r = min(n_new, CHUNK)
        mix_s = jax.vmap(lambda w: jnp.kron(jnp.eye(CHUNK // r, dtype=F32), w))(jnp.tril(sgu_w[l, :, :r, :r])).astype(BF16)
        mb_s = jnp.tile(jnp.repeat(sgu_b[l, :, :r].T, GMLP_DIM, axis=1), (CHUNK // r, 1))

        hp = _ffn_ln(hp, *f1, alpha)
        (qpad, cmpk_p, cmpv_p, _, _, cmp_t, slc_t, win_t, ksa, vsd, kwp, vwd, gate, ogn, _) = _proj(
            hp, wp, sg, sb, mix_p, mb_p, go_g, seq)
        ckv = _compress_prompt(cmpk_p, cmpv_p, nb, wab, pos16, b1, w2p)
        oan = _nsa_prompt(qpad, gate, ckv, ksa, vsd, kwp, vwd, ov_p, go_a, nb, seq)
        hp = _ffn_ln(hp, *f2, alpha, merge_args=(oan, ogn, wm, row(ln2_g[l]), row(ln2_b[l])))
        outs[0].append(_token_major(cmp_t, nb))
        outs[2].append(_token_major(slc_t, nb))
        outs[4].append(_token_major(win_t[:, :, seq - min(WINDOW, seq):], nb))

        hs = _ffn_ln(hs, *f1, alpha)
        (qpad, cmpk_s, cmpv_s, slc_s, win_s, _, _, _, _, _, _, _, gate, ogn, v_s) = _proj(
            hs, wp, sg, sb, mix_s, mb_s, go_g, ns * n_new)
        cmp_s = jnp.concatenate([cmpk_s, cmpv_s], axis=1)
        ckv = _compress_sample(page_table, _feature_major(cache_cmp_kv[l]), wab, pos16, b1, w2s)
        q5 = qpad.reshape(ns, n_new, N_KV_HEADS, GQA, LANES)[..., :HEAD_DIM]
        q5 = jnp.pad(jnp.transpose(q5, (0, 2, 3, 1, 4)), ((0, 0), (0, 0), (0, 0), (0, T8 - n_new), (0, 0)))
        zq = jnp.zeros_like(q5[:, 0])
        q_bd = jnp.stack([jnp.concatenate([q5[:, 0], zq], axis=-1), jnp.concatenate([zq, q5[:, 1]], axis=-1)],
                         axis=1).reshape(ns, N_KV_HEADS * GQA * T8, LANES)
        g5 = gate[:, :3 * N_HEADS].reshape(ns, n_new, 3, N_KV_HEADS, GQA)
        g5 = jnp.pad(jnp.transpose(g5, (0, 3, 4, 1, 2)), ((0, 0), (0, 0), (0, 0), (0, T8 - n_new), (0, 0)))
        gate_rows = g5.reshape(ns, N_KV_HEADS * GQA * T8, 3)
        new8 = jnp.pad(jnp.concatenate([cmp_s, slc_s, win_s], axis=1).reshape(ns, n_new, 6 * KV_W),
                       ((0, 0), (0, T8 - n_new), (0, 0)))
        oan = _nsa_sample(page_table, _feature_major(cache_slc_kv[l]), q_bd, gate_rows, ckv,
                          _feature_major(state_win_kv[l]), new8, e_mat, ov_s, go_a, past, n_new)
        oan = oan[:, :n_new].reshape(ns * n_new, ATTN_W)
        hs = _ffn_ln(hs, *f2, alpha, merge_args=(oan, ogn, wm, row(ln2_g[l]), row(ln2_b[l])))
        win_all = jnp.concatenate([state_win_kv[l], win_s.reshape(ns, n_new, 2, *kv5)], axis=1)
        outs[1].append(cmp_s.reshape(ns, n_new, 2, *kv5))
        outs[3].append(slc_s.reshape(ns, n_new, 2, *kv5))
        outs[5].append(win_all[:, -min(WINDOW, past + n_new):])
        outs[6].append(v_s.reshape(ns, n_new, N_GMLP, GMLP_DIM))
    st = [jnp.stack(o, axis=0) for o in outs]
    return (hp.reshape(nb, seq, d), hs.reshape(ns, n_new, d), st[0], st[1], st[2], st[3], st[4], st[5], st[6])
```
